```python
import math
import jax
import jax.numpy as jnp
from jax import lax
import numpy as np

D_MODEL = 1024
BATCH = 8
SEQ = 8192
DEPTH = 2
DEC_BATCH = 16
DEC_SEQ = 64
PAST_LEN = 4096

CHUNK = 64
EPS = 1e-6

GLA_HEADS = 4
GLA_DK = D_MODEL // 8
GLA_DV = D_MODEL // 4
GLA_RANK = 16
GLA_TAU = 16.0

SWA_HEADS = 16
SWA_KV_HEADS = 4
SWA_GROUP = SWA_HEADS // SWA_KV_HEADS
SWA_HD = 64
WINDOW = 128
WIN_CHUNKS = WINDOW // CHUNK

REL_BUCKETS = 32
REL_MAX_DIST = 128

CMLP_CHUNK = 128
CMLP_GROUPS = 4
CMLP_WIDTH = D_MODEL
CMLP_GW = CMLP_WIDTH // CMLP_GROUPS

MOE_GROUPS = 4
MOE_PER_GROUP = 8
N_EXPERTS = MOE_GROUPS * MOE_PER_GROUP
MOE_TOPK = 2
MOE_HIDDEN = D_MODEL // 4
MOE_BLOCK = 128

W_QA = GLA_HEADS * GLA_DK
W_KA = GLA_HEADS * GLA_DK
W_VA = GLA_HEADS * GLA_DV
W_GA = GLA_RANK
W_RA = GLA_HEADS * GLA_DV
W_QB = SWA_HEADS * SWA_HD
W_KB = SWA_KV_HEADS * SWA_HD
W_VB = SWA_KV_HEADS * SWA_HD
W_ZC = 2 * CMLP_WIDTH
IN_WIDTHS = (W_QA, W_KA, W_VA, W_GA, W_RA, W_QB, W_KB, W_VB, W_ZC)
IN_SPLITS = tuple(sum(IN_WIDTHS[:i + 1]) for i in range(len(IN_WIDTHS) - 1))
D_IN = sum(IN_WIDTHS)

kernel_name = 'hybrid_stream_encoder_step'


def rmsnorm(x, g):
    xf = x.astype(jnp.float32)
    y = xf * lax.rsqrt(jnp.mean(xf * xf, axis=-1, keepdims=True) + EPS)
    return (y * g.astype(jnp.float32)).astype(x.dtype)


def layernorm(x, g, b):
    xf = x.astype(jnp.float32)
    xc = xf - jnp.mean(xf, axis=-1, keepdims=True)
    var = jnp.mean(xc * xc, axis=-1, keepdims=True)
    return xc * lax.rsqrt(var + EPS) * g.astype(jnp.float32) + b.astype(jnp.float32)


def modulate(x, g, shift, scale):
    return rmsnorm(x, g) * (1 + scale[:, None, :]) + shift[:, None, :]


def rel_bucket(rel):
    nb = REL_BUCKETS // 2
    max_exact = nb // 2
    ret = jnp.where(rel > 0, nb, 0)
    n = jnp.abs(rel)
    nf = jnp.maximum(n, 1).astype(jnp.float32)
    large = max_exact + (jnp.log(nf / max_exact) / math.log(REL_MAX_DIST / max_exact)
                         * (nb - max_exact)).astype(jnp.int32)
    large = jnp.minimum(large, nb - 1)
    return ret + jnp.where(n < max_exact, n, large)


def rel_bias(table, n_q, n_k, n_before):
    rel = jnp.arange(n_k)[None, :] - n_before - jnp.arange(n_q)[:, None]
    b = jnp.transpose(table[rel_bucket(rel)].astype(jnp.float32), (2, 0, 1))
    return b.reshape(SWA_KV_HEADS, SWA_GROUP, n_q, n_k)


def heads(t, n):
    B, T, _ = t.shape
    return t.reshape(B, T, n, -1).transpose(0, 2, 1, 3).astype(jnp.float32)


def gla_chunk(S, q, k, v, la):
    C = q.shape[2]
    b = jnp.cumsum(la, axis=2)
    qt = q * jnp.exp(b)
    kt = k * jnp.exp(-b)
    causal = jnp.tril(jnp.ones((C, C), bool))
    A = jnp.where(causal, jnp.einsum('bhtd,bhsd->bhts', qt, kt), 0.0)
    o = jnp.einsum('bhts,bhsv->bhtv', A, v) + jnp.einsum('bhtd,bhdv->bhtv', qt, S)
    b_last = b[:, :, -1:, :]
    S_new = (jnp.exp(b_last[:, :, 0, :])[..., None] * S
             + jnp.einsum('bhsd,bhsv->bhdv', k * jnp.exp(b_last - b), v))
    return S_new, o


def gla_prompt(q, k, v, la):
    B, H, S, _ = q.shape
    n = S // CHUNK

    def to_blocks(t):
        return t.reshape(B, H, n, CHUNK, t.shape[-1]).transpose(2, 0, 1, 3, 4)

    s0 = jnp.zeros((B, H, GLA_DK, GLA_DV), jnp.float32)
    s_fin, o = lax.scan(lambda s, a: gla_chunk(s, *a), s0,
                        (to_blocks(q), to_blocks(k), to_blocks(v), to_blocks(la)))
    return o.transpose(1, 2, 0, 3, 4).reshape(B, H, S, GLA_DV), s_fin


def swa_core(q, k, v, bias, valid, sink):
    s = jnp.einsum('ncxgd,nlxd->nxgcl', q, k).astype(jnp.float32) * (SWA_HD ** -0.5) + bias
    s = jnp.where(valid[:, None, None, None, :], s, -1e30)
    sk = jnp.broadcast_to(sink.astype(jnp.float32).reshape(SWA_KV_HEADS, SWA_GROUP)[None, :, :, None, None],
                          s.shape[:-1] + (1,))
    p = jax.nn.softmax(jnp.concatenate([s, sk], axis=-1), axis=-1)[..., :-1]
    return jnp.einsum('nxgcl,nlxd->ncxgd', p.astype(v.dtype), v)


def swa_prompt(q, k, v, table, sink):
    B, S = q.shape[:2]
    N = S // CHUNK
    L = WINDOW + CHUNK
    qc = q.reshape(B, N, CHUNK, SWA_KV_HEADS, SWA_GROUP, SWA_HD)

    def windows(t):
        tp = jnp.pad(t, ((0, 0), (WINDOW, 0), (0, 0), (0, 0)))
        tp = tp.reshape(B, N + WIN_CHUNKS, CHUNK, SWA_KV_HEADS, SWA_HD)
        return jnp.concatenate([tp[:, i:i + N] for i in range(WIN_CHUNKS + 1)], axis=2)

    kw, vw = windows(k), windows(v)
    key_pos = (jnp.arange(N)[:, None] - WIN_CHUNKS) * CHUNK + jnp.arange(L)[None, :]
    valid = key_pos >= 0
    bias = rel_bias(table, CHUNK, L, WINDOW)
    o = lax.map(lambda a: swa_core(a[0], a[1], a[2], bias, valid, sink), (qc, kw, vw))
    return o.reshape(B, S, SWA_HEADS * SWA_HD)


def swa_sample(q, k_new, v_new, k_cache, v_cache, table, sink):
    Bd, T = q.shape[:2]
    Lc = k_cache.shape[1]
    kw = jnp.concatenate([k_cache.astype(k_new.dtype), k_new], axis=1)
    vw = jnp.concatenate([v_cache.astype(v_new.dtype), v_new], axis=1)
    qc = q.reshape(Bd, T, SWA_KV_HEADS, SWA_GROUP, SWA_HD)
    bias = rel_bias(table, T, Lc + T, Lc)
    valid = jnp.ones((Bd, Lc + T), bool)
    o = swa_core(qc, kw, vw, bias, valid, sink)
    return o.reshape(Bd, T, SWA_HEADS * SWA_HD), kw[:, -Lc:], vw[:, -Lc:]


def cmlp(zc, ln_g, ln_b, ws, bs):
    B, T, _ = zc.shape
    z = jax.nn.gelu(zc.astype(jnp.float32))
    u, v = jnp.split(z, 2, axis=-1)
    v = layernorm(v, ln_g, ln_b)
    n = min(T, CMLP_CHUNK)
    pos = jnp.arange(n)
    mask = (pos[None, :] // CHUNK) <= (pos[:, None] // CHUNK)
    w = jnp.where(mask[None], ws[:, :n, :n].astype(jnp.float32), 0.0)
    vc = v.reshape(B, T // n, n, CMLP_GROUPS, CMLP_GW)
    mix = jnp.einsum('gij,bnjgc->bnigc', w, vc) + bs[:, :n].T.astype(jnp.float32)[:, :, None]
    out = (u * mix.reshape(B, T, CMLP_WIDTH)).astype(zc.dtype)
    return out, v.astype(zc.dtype)


def grouped_experts(xf, eidx, ew, w_gate, w_up, w_down):
    T, D = xf.shape
    n_assign = T * MOE_TOPK
    flat_e = eidx.reshape(-1).astype(jnp.int32)
    order = jnp.argsort(flat_e)
    sorted_e = flat_e[order]
    counts = jnp.zeros((N_EXPERTS,), jnp.int32).at[flat_e].add(1)
    padded = (counts + MOE_BLOCK - 1) // MOE_BLOCK * MOE_BLOCK
    pad_end = jnp.cumsum(padded)
    pad_start = pad_end - padded
    seg_start = jnp.cumsum(counts) - counts
    dest = pad_start[sorted_e] + jnp.arange(n_assign, dtype=jnp.int32) - seg_start[sorted_e]
    n_blocks = -(-n_assign // MOE_BLOCK) + N_EXPERTS
    tok = (order // MOE_TOPK).astype(jnp.int32)
    slot_tok = jnp.zeros((n_blocks * MOE_BLOCK,), jnp.int32).at[dest].set(tok)
    block_e = jnp.minimum(jnp.searchsorted(pad_end, jnp.arange(n_blocks, dtype=jnp.int32) * MOE_BLOCK,
                                           side='right'), N_EXPERTS - 1)

    def expert_block(args):
        toks, e = args
        xb = xf[toks]
        hid = jax.nn.silu(xb @ w_gate[e]) * (xb @ w_up[e])
        return hid @ w_down[e]

    yb = lax.map(expert_block, (slot_tok.reshape(n_blocks, MOE_BLOCK), block_e)).reshape(-1, D)
    contrib = yb[dest] * ew.reshape(-1)[order][:, None].astype(yb.dtype)
    return jnp.zeros_like(xf).at[tok].add(contrib)


def hmoe(h, lp):
    B, T, D = h.shape
    xf = h.reshape(B * T, D)
    g_logit = (xf @ lp['moe_w_rg']).astype(jnp.float32) + lp['moe_b_rg'].astype(jnp.float32)
    g_prob = jax.nn.softmax(g_logit, axis=-1)
    g_val, g_idx = lax.top_k(g_prob, 1)
    e_logit = ((xf @ lp['moe_w_re']).astype(jnp.float32) + lp['moe_b_re'].astype(jnp.float32))
    e_logit = e_logit.reshape(B * T, MOE_GROUPS, MOE_PER_GROUP)[jnp.arange(B * T), g_idx[:, 0]]
    e_top, e_idx = lax.top_k(e_logit, MOE_TOPK)
    weights = jax.nn.softmax(e_top, axis=-1) * g_val
    experts = g_idx * MOE_PER_GROUP + e_idx
    y = grouped_experts(xf, experts, weights, lp['moe_w_gate'], lp['moe_w_up'], lp['moe_w_down'])
    return y.reshape(B, T, D)


def token_mixers(h, lp, rel_table, is_prompt, gla_s, swa_kc, swa_vc):
    B, T, _ = h.shape
    qa, ka, va, ga, ra, qb, kb, vb, zc = jnp.split(h @ lp['w_in'], IN_SPLITS, axis=-1)
    la = jax.nn.log_sigmoid((ga @ lp['gla_w_gate'] + lp['gla_b_gate']).astype(jnp.float32)) / GLA_TAU
    q_a = heads(qa, GLA_HEADS) * (GLA_DK ** -0.5)
    k_a, v_a, la_h = heads(ka, GLA_HEADS), heads(va, GLA_HEADS), heads(la, GLA_HEADS)
    if is_prompt:
        o_a, s_new = gla_prompt(q_a, k_a, v_a, la_h)
    else:
        s_new, o_a = gla_chunk(gla_s.astype(jnp.float32), q_a, k_a, v_a, la_h)
    o_a = rmsnorm(o_a.transpose(0, 2, 1, 3), lp['gla_norm_g']).reshape(B, T, W_VA)
    o_a = (o_a * jax.nn.silu(ra.astype(jnp.float32))).astype(h.dtype)
    q_b = qb.reshape(B, T, SWA_HEADS, SWA_HD)
    k_b = kb.reshape(B, T, SWA_KV_HEADS, SWA_HD)
    v_b = vb.reshape(B, T, SWA_KV_HEADS, SWA_HD)
    if is_prompt:
        o_b = swa_prompt(q_b, k_b, v_b, rel_table, lp['swa_sink'])
        k_state, v_state = k_b[:, -WINDOW:], v_b[:, -WINDOW:]
    else:
        o_b, k_state, v_state = swa_sample(q_b, k_b, v_b, swa_kc, swa_vc, rel_table, lp['swa_sink'])
    o_c, v_c = cmlp(zc, lp['cmlp_ln_g'], lp['cmlp_ln_b'], lp['cmlp_ws'], lp['cmlp_bs'])
    g_a, g_b, g_c = jnp.split(jax.nn.sigmoid(h @ lp['w_merge_gate'] + lp['b_merge_gate']), 3, axis=-1)
    m = (g_a * (o_a @ lp['w_branch_a']) + g_b * (o_b @ lp['w_branch_b'])
         + g_c * (o_c @ lp['w_branch_c']))
    return m @ lp['w_out'], s_new, k_state, v_state, v_c


def run_group(x, c, stacked, rel_table, final_norm_g, is_prompt, state_gla, cache_k, cache_v):
    gla_out, k_out, v_out, vc_out = [], [], [], []
    for l in range(DEPTH):
        lp = {name: arr[l] for name, arr in stacked.items()}
        mod = jax.nn.silu(c) @ lp['w_mod'] + lp['b_mod']
        sh1, sc1, gt1, sh2, sc2, gt2 = jnp.split(mod, 6, axis=-1)
        h = modulate(x, lp['norm1_g'], sh1, sc1)
        if is_prompt:
            mix, s_new, k_new, v_new, v_c = token_mixers(h, lp, rel_table, True, None, None, None)
        else:
            mix, s_new, k_new, v_new, v_c = token_mixers(h, lp, rel_table, False,
                                                         state_gla[l], cache_k[l], cache_v[l])
            vc_out.append(v_c)
        x = x + gt1[:, None, :] * mix
        h2 = modulate(x, lp['norm2_g'], sh2, sc2)
        x = x + gt2[:, None, :] * hmoe(h2, lp)
        gla_out.append(s_new)
        k_out.append(k_new)
        v_out.append(v_new)
    vc_stack = None if is_prompt else jnp.stack(vc_out)
    return rmsnorm(x, final_norm_g), jnp.stack(gla_out), jnp.stack(k_out), jnp.stack(v_out), vc_stack


def setup_inputs(seed: int = 0) -> dict:
    key = jax.random.key(seed)
    ks = iter(jax.random.split(key, 40))

    def nrm(shape, scale):
        return jax.random.normal(next(ks), shape, jnp.float32) * scale

    L = min(WINDOW, PAST_LEN)
    D = D_MODEL
    return {
        'x_prompt': nrm((BATCH, SEQ, D), 1.0),
        'x_sample': nrm((DEC_BATCH, DEC_SEQ, D), 1.0),
        'state_gla': nrm((DEPTH, DEC_BATCH, GLA_HEADS, GLA_DK, GLA_DV), 1.0),
        'cache_swa_k': nrm((DEPTH, DEC_BATCH, L, SWA_KV_HEADS, SWA_HD), 1.0),
        'cache_swa_v': nrm((DEPTH, DEC_BATCH, L, SWA_KV_HEADS, SWA_HD), 1.0),
        'c_prompt': nrm((BATCH, D), 1.0),
        'c_sample': nrm((DEC_BATCH, D), 1.0),
        'rel_table': nrm((REL_BUCKETS, SWA_HEADS), 0.2),
        'final_norm_g': 1.0 + nrm((D,), 0.01),
        'norm1_g': 1.0 + nrm((DEPTH, D), 0.01),
        'norm2_g': 1.0 + nrm((DEPTH, D), 0.01),
        'w_mod': nrm((DEPTH, D, 6 * D), 0.5 * D ** -0.5),
        'b_mod': nrm((DEPTH, 6 * D), 0.01),
        'w_in': nrm((DEPTH, D, D_IN), D ** -0.5),
        'gla_w_gate': nrm((DEPTH, GLA_RANK, W_QA), GLA_RANK ** -0.5),
        'gla_b_gate': 1.0 + nrm((DEPTH, W_QA), 0.1),
        'gla_norm_g': 1.0 + nrm((DEPTH, GLA_DV), 0.01),
        'swa_sink': nrm((DEPTH, SWA_HEADS), 0.5),
        'cmlp_ln_g': 1.0 + nrm((DEPTH, CMLP_WIDTH), 0.01),
        'cmlp_ln_b': nrm((DEPTH, CMLP_WIDTH), 0.01),
        'cmlp_ws': nrm((DEPTH, CMLP_GROUPS, CMLP_CHUNK, CMLP_CHUNK), CMLP_CHUNK ** -0.5),
        'cmlp_bs': 1.0 + nrm((DEPTH, CMLP_GROUPS, CMLP_CHUNK), 0.01),
        'w_branch_a': nrm((DEPTH, W_VA, D), W_VA ** -0.5),
        'w_branch_b': nrm((DEPTH, W_QB, D), W_QB ** -0.5),
        'w_branch_c': nrm((DEPTH, CMLP_WIDTH, D), CMLP_WIDTH ** -0.5),
        'w_merge_gate': nrm((DEPTH, D, 3 * D), D ** -0.5),
        'b_merge_gate': nrm((DEPTH, 3 * D), 0.01),
        'w_out': nrm((DEPTH, D, D), D ** -0.5),
        'moe_w_rg': nrm((DEPTH, D, MOE_GROUPS), D ** -0.5),
        'moe_b_rg': nrm((DEPTH, MOE_GROUPS), 0.01),
        'moe_w_re': nrm((DEPTH, D, N_EXPERTS), D ** -0.5),
        'moe_b_re': nrm((DEPTH, N_EXPERTS), 0.01),
        'moe_w_gate': nrm((DEPTH, N_EXPERTS, D, MOE_HIDDEN), D ** -0.5),
        'moe_w_up': nrm((DEPTH, N_EXPERTS, D, MOE_HIDDEN), D ** -0.5),
        'moe_w_down': nrm((DEPTH, N_EXPERTS, MOE_HIDDEN, D), MOE_HIDDEN ** -0.5),
    }


def reference(x_prompt, x_sample, state_gla, cache_swa_k, cache_swa_v, c_prompt, c_sample,
              rel_table, final_norm_g, norm1_g, norm2_g, w_mod, b_mod, w_in,
              gla_w_gate, gla_b_gate, gla_norm_g, swa_sink, cmlp_ln_g, cmlp_ln_b, cmlp_ws, cmlp_bs,
              w_branch_a, w_branch_b, w_branch_c, w_merge_gate, b_merge_gate, w_out,
              moe_w_rg, moe_b_rg, moe_w_re, moe_b_re, moe_w_gate, moe_w_up, moe_w_down):
    stacked = dict(norm1_g=norm1_g, norm2_g=norm2_g, w_mod=w_mod, b_mod=b_mod, w_in=w_in,
                   gla_w_gate=gla_w_gate, gla_b_gate=gla_b_gate, gla_norm_g=gla_norm_g,
                   swa_sink=swa_sink, cmlp_ln_g=cmlp_ln_g, cmlp_ln_b=cmlp_ln_b,
                   cmlp_ws=cmlp_ws, cmlp_bs=cmlp_bs, w_branch_a=w_branch_a, w_branch_b=w_branch_b,
                   w_branch_c=w_branch_c, w_merge_gate=w_merge_gate, b_merge_gate=b_merge_gate,
                   w_out=w_out, moe_w_rg=moe_w_rg, moe_b_rg=moe_b_rg, moe_w_re=moe_w_re,
                   moe_b_re=moe_b_re, moe_w_gate=moe_w_gate, moe_w_up=moe_w_up, moe_w_down=moe_w_down)
    y_prompt, gla_state_prompt, swa_k_prompt, swa_v_prompt, _ = run_group(
        x_prompt, c_prompt, stacked, rel_table, final_norm_g, True, None, None, None)
    y_sample, gla_state_sample, swa_k_sample, swa_v_sample, cmlp_v_sample = run_group(
        x_sample, c_sample, stacked, rel_table, final_norm_g, False, state_gla, cache_swa_k, cache_swa_v)
    return (y_prompt, y_sample, gla_state_prompt, swa_k_prompt, swa_v_prompt,
            gla_state_sample, swa_k_sample, swa_v_sample, cmlp_v_sample)
```

```python
import functools

import jax
import jax.numpy as jnp
from jax import lax
from jax.experimental import pallas as pl
from jax.experimental.pallas import tpu as pltpu

F32 = jnp.float32
BF16 = jnp.bfloat16
HIGHEST = lax.Precision.HIGHEST

D_MODEL = 1024
CHUNK = 64
EPS = 1e-6
GLA_HEADS = 4
GLA_DK = 128
GLA_DV = 256
GLA_RANK = 16
GLA_TAU = 16.0
SWA_HEADS = 16
SWA_KV_HEADS = 4
SWA_GROUP = 4
SWA_HD = 64
WINDOW = 128
REL_BUCKETS = 32
REL_MAX_DIST = 128
CMLP_CHUNK = 128
CMLP_GROUPS = 4
CMLP_WIDTH = 1024
CMLP_GW = 256
MOE_GROUPS = 4
MOE_PER_GROUP = 8
N_EXPERTS = 32
MOE_HIDDEN = 256

LANES = 128
SUBLANES = 8
VMEM_LIMIT = 56 * 1024 * 1024

TM_PROJ = 512
TB_GLA = 512
TQ_SWA = 256
TM_CMLP = 256
TM_MOE = 1024
C_MOE = 128
ROUTE_W = 128


def _cparams(sem):
    return pltpu.CompilerParams(dimension_semantics=sem, vmem_limit_bytes=VMEM_LIMIT)


def _dot(a, b):
    return jnp.dot(a, b, preferred_element_type=F32)


def _dot_nt(a, b):
    return lax.dot_general(a, b, (((1,), (1,)), ((), ())), preferred_element_type=F32)


def _dot_tn(a, b):
    return lax.dot_general(a, b, (((0,), (0,)), ((), ())), preferred_element_type=F32)


def _sigmoid(x):
    return 1.0 / (1.0 + jnp.exp(-x))


def _modulate(x, g, shift, scale):
    ms = jnp.mean(x * x, axis=-1, keepdims=True)
    return x * lax.rsqrt(ms + EPS) * g * (1.0 + scale) + shift


def _mod_kernel(c_ref, w_ref, b_ref, o_ref):
    c = c_ref[...]
    s = c * _sigmoid(c)
    o_ref[0] = jnp.dot(s, w_ref[0], precision=HIGHEST, preferred_element_type=F32) + b_ref[0]


def _mod_call(c_all, w_mod, b_mod):
    depth, d, n = w_mod.shape
    r = c_all.shape[0]
    bn = 1024
    return pl.pallas_call(
        _mod_kernel,
        grid=(depth, n // bn),
        in_specs=[
            pl.BlockSpec((r, d), lambda l, j: (0, 0)),
            pl.BlockSpec((1, d, bn), lambda l, j: (l, 0, j)),
            pl.BlockSpec((1, 1, bn), lambda l, j: (l, 0, j)),
        ],
        out_specs=pl.BlockSpec((1, r, bn), lambda l, j: (l, 0, j)),
        out_shape=jax.ShapeDtypeStruct((depth, r, n), F32),
        compiler_params=_cparams(("arbitrary", "arbitrary")),
        name="mod",
    )(c_all, w_mod, b_mod.reshape(depth, 1, n))


def _inproj_kernel(x_ref, sh_ref, sc_ref, g_ref, wa_ref, wq_ref, wkv_ref, wz_ref, wg_ref,
                   pa_ref, qb_ref, kv_ref, zc_ref, ga_ref):
    h = _modulate(x_ref[...], g_ref[...], sh_ref[...], sc_ref[...]).astype(BF16)
    col = 512
    for w_ref, o_ref in ((wa_ref, pa_ref), (wq_ref, qb_ref), (wkv_ref, kv_ref), (wz_ref, zc_ref)):
        n = w_ref.shape[1]
        for j in range(0, n, col):
            o_ref[:, j:j + col] = _dot(h, w_ref[:, j:j + col]).astype(o_ref.dtype)
    ga_ref[...] = _dot(h, wg_ref[...])


def _row_spec(tm, width, per_seq_tiles):
    if per_seq_tiles is None:
        return pl.BlockSpec((tm, width), lambda i: (i, 0))
    return pl.BlockSpec((None, 1, width), lambda i: (i // per_seq_tiles, 0, 0))


def _const_spec(shape):
    nd = len(shape)
    return pl.BlockSpec(shape, lambda i: (0,) * nd, pipeline_mode=pl.Buffered(1))


def _inproj_call(x, sh, sc, g, wa, wq, wkv, wz, wg, tm, per_seq_tiles):
    n, d = x.shape
    outs = (
        jax.ShapeDtypeStruct((n, wa.shape[1]), BF16),
        jax.ShapeDtypeStruct((n, wq.shape[1]), BF16),
        jax.ShapeDtypeStruct((n, wkv.shape[1]), F32),
        jax.ShapeDtypeStruct((n, wz.shape[1]), BF16),
        jax.ShapeDtypeStruct((n, LANES), F32),
    )
    return pl.pallas_call(
        _inproj_kernel,
        grid=(n // tm,),
        in_specs=[
            pl.BlockSpec((tm, d), lambda i: (i, 0)),
            _row_spec(tm, d, per_seq_tiles),
            _row_spec(tm, d, per_seq_tiles),
            _const_spec((1, d)),
            _const_spec(wa.shape), _const_spec(wq.shape), _const_spec(wkv.shape),
            _const_spec(wz.shape), _const_spec(wg.shape),
        ],
        out_specs=[pl.BlockSpec((tm, o.shape[1]), lambda i: (i, 0)) for o in outs],
        out_shape=outs,
        compiler_params=_cparams(("arbitrary",)),
        name="inproj",
    )(x, sh, sc, g, wa, wq, wkv, wz, wg)


def _gla_kernel(*refs, nblk, has_state):
    if has_state:
        (q_ref, k_ref, v_ref, r_ref, ga_ref, wg_ref, bg_ref, ng_ref, s0_ref,
         o_ref, sfin_ref, st_ref) = refs
    else:
        (q_ref, k_ref, v_ref, r_ref, ga_ref, wg_ref, bg_ref, ng_ref,
         o_ref, sfin_ref, st_ref) = refs
        s0_ref = None
    t = pl.program_id(2)

    @pl.when(t == 0)
    def _():
        if has_state:
            st_ref[...] = s0_ref[0, 0].T
        else:
            st_ref[...] = jnp.zeros_like(st_ref)

    row = lax.broadcasted_iota(jnp.int32, (CHUNK, CHUNK), 0)
    colm = lax.broadcasted_iota(jnp.int32, (CHUNK, CHUNK), 1)
    causal = row >= colm
    tril = causal.astype(F32)
    wg = wg_ref[...].astype(BF16)
    for c in range(nblk):
        sl = slice(c * CHUNK, (c + 1) * CHUNK)
        gl = _dot(ga_ref[sl, :].astype(BF16), wg) + bg_ref[...]
        la = (jnp.minimum(gl, 0.0) - jnp.log(1.0 + jnp.exp(-jnp.abs(gl)))) / GLA_TAU
        b = jnp.dot(tril, la, precision=HIGHEST, preferred_element_type=F32)
        bl = b[CHUNK - 1:CHUNK, :]
        q = q_ref[sl, :].astype(F32) * (GLA_DK ** -0.5)
        k = k_ref[sl, :].astype(F32)
        qt = (q * jnp.exp(b)).astype(BF16)
        kt = (k * jnp.exp(-b)).astype(BF16)
        kd = (k * jnp.exp(bl - b)).astype(BF16)
        v = v_ref[sl, :]
        a = jnp.where(causal, _dot_nt(qt, kt), 0.0).astype(BF16)
        st = st_ref[...]
        o = _dot(a, v) + _dot_nt(qt, st.astype(BF16))
        st_ref[...] = st * jnp.exp(bl) + _dot_tn(v, kd)
        on = o * lax.rsqrt(jnp.mean(o * o, axis=-1, keepdims=True) + EPS) * ng_ref[...]
        r = r_ref[sl, :].astype(F32)
        o_ref[sl, :] = (on * (r * _sigmoid(r))).astype(o_ref.dtype)

    @pl.when(t == pl.num_programs(2) - 1)
    def _():
        sfin_ref[0, 0] = st_ref[...].T


def _gla_call(pa, ga, wg, bg, ng, s0, nseq, seqlen, tb):
    n = pa.shape[0]
    nt = seqlen // tb
    kq = GLA_HEADS
    vq = (2 * GLA_HEADS * GLA_DK) // GLA_DV
    rq = vq + GLA_HEADS
    in_specs = [
        pl.BlockSpec((tb, GLA_DK), lambda b, h, t: (b * nt + t, h)),
        pl.BlockSpec((tb, GLA_DK), lambda b, h, t: (b * nt + t, kq + h)),
        pl.BlockSpec((tb, GLA_DV), lambda b, h, t: (b * nt + t, vq + h)),
        pl.BlockSpec((tb, GLA_DV), lambda b, h, t: (b * nt + t, rq + h)),
        pl.BlockSpec((tb, LANES), lambda b, h, t: (b * nt + t, 0)),
        pl.BlockSpec((LANES, GLA_DK), lambda b, h, t: (0, h)),
        pl.BlockSpec((1, GLA_DK), lambda b, h, t: (0, h)),
        pl.BlockSpec((1, GLA_DV), lambda b, h, t: (0, 0)),
    ]
    args = [pa, pa, pa, pa, ga, wg, bg, ng]
    if s0 is not None:
        in_specs.append(pl.BlockSpec((1, 1, GLA_DK, GLA_DV), lambda b, h, t: (b, h, 0, 0)))
        args.append(s0)
    return pl.pallas_call(
        functools.partial(_gla_kernel, nblk=tb // CHUNK, has_state=s0 is not None),
        grid=(nseq, GLA_HEADS, nt),
        in_specs=in_specs,
        out_specs=[
            pl.BlockSpec((tb, GLA_DV), lambda b, h, t: (b * nt + t, h)),
            pl.BlockSpec((1, 1, GLA_DK, GLA_DV), lambda b, h, t: (b, h, 0, 0)),
        ],
        out_shape=(
            jax.ShapeDtypeStruct((n, GLA_HEADS * GLA_DV), BF16),
            jax.ShapeDtypeStruct((nseq, GLA_HEADS, GLA_DK, GLA_DV), F32),
        ),
        scratch_shapes=[pltpu.VMEM((GLA_DV, GLA_DK), F32)],
        compiler_params=_cparams(("arbitrary", "arbitrary", "arbitrary")),
        name="gla",
    )(*args)


def _swa_kernel(q_ref, kc_ref, vc_ref, kp_ref, vp_ref, bias_ref, sink_ref, o_ref, *, tq, mask_start):
    blk = pl.program_id(1)
    kall = jnp.concatenate([kp_ref[...], kc_ref[...]], axis=0).astype(BF16)
    vall = jnp.concatenate([vp_ref[...], vc_ref[...]], axis=0).astype(BF16)
    lwin = WINDOW + CHUNK
    rows = SWA_GROUP * CHUNK
    for c in range(tq // CHUNK):
        if mask_start:
            first_pos = blk * tq + c * CHUNK - WINDOW
            valid = lax.broadcasted_iota(jnp.int32, (rows, lwin), 1) + first_pos >= 0
        for x in range(SWA_KV_HEADS):
            kw = kall[c * CHUNK:c * CHUNK + lwin, x * SWA_HD:(x + 1) * SWA_HD]
            vw = vall[c * CHUNK:c * CHUNK + lwin, x * SWA_HD:(x + 1) * SWA_HD]
            base = x * SWA_GROUP * SWA_HD
            qs = jnp.concatenate(
                [q_ref[c * CHUNK:(c + 1) * CHUNK, base + g * SWA_HD:base + (g + 1) * SWA_HD]
                 for g in range(SWA_GROUP)], axis=0)
            s = _dot_nt(qs, kw) * (SWA_HD ** -0.5) + bias_ref[x]
            if mask_start:
                s = jnp.where(valid, s, -1e30)
            sink = sink_ref[x]
            m = jnp.maximum(jnp.max(s, axis=-1, keepdims=True), sink)
            p = jnp.exp(s - m)
            den = jnp.sum(p, axis=-1, keepdims=True) + jnp.exp(sink - m)
            o = _dot(p.astype(BF16), vw) / den
            o_ref[c * CHUNK:(c + 1) * CHUNK, base:base + SWA_GROUP * SWA_HD] = jnp.concatenate(
                [o[g * CHUNK:(g + 1) * CHUNK, :] for g in range(SWA_GROUP)], axis=1).astype(o_ref.dtype)


def _swa_call(qb, k_cur, v_cur, k_prev, v_prev, bias, sink, nseq, seqlen, tq, prompt):
    n = qb.shape[0]
    nt = seqlen // tq
    kvw = SWA_KV_HEADS * SWA_HD
    per_prev = tq // WINDOW if prompt else None

    def cur_spec(cb):
        return pl.BlockSpec((tq, kvw), lambda b, t: (b * nt + t, cb))

    def prev_spec(cb):
        if prompt:
            nprev = seqlen // WINDOW
            return pl.BlockSpec((WINDOW, kvw),
                                lambda b, t: (b * nprev + jnp.maximum(t * per_prev - 1, 0), cb))
        return pl.BlockSpec((None, WINDOW, kvw), lambda b, t: (b, 0, cb))

    return pl.pallas_call(
        functools.partial(_swa_kernel, tq=tq, mask_start=prompt),
        grid=(nseq, nt),
        in_specs=[
            pl.BlockSpec((tq, SWA_HEADS * SWA_HD), lambda b, t: (b * nt + t, 0)),
            cur_spec(k_cur[1]), cur_spec(v_cur[1]), prev_spec(k_prev[1]), prev_spec(v_prev[1]),
            pl.BlockSpec(bias.shape, lambda b, t: (0, 0, 0)),
            pl.BlockSpec(sink.shape, lambda b, t: (0, 0, 0)),
        ],
        out_specs=pl.BlockSpec((tq, SWA_HEADS * SWA_HD), lambda b, t: (b * nt + t, 0)),
        out_shape=jax.ShapeDtypeStruct((n, SWA_HEADS * SWA_HD), BF16),
        compiler_params=_cparams(("arbitrary", "arbitrary")),
        name="swa",
    )(qb, k_cur[0], v_cur[0], k_prev[0], v_prev[0], bias, sink)


def _cmlp_kernel(*refs, n, emit_v):
    if emit_v:
        z_ref, lg_ref, lb_ref, w_ref, bs_ref, o_ref, v_ref = refs
    else:
        z_ref, lg_ref, lb_ref, w_ref, bs_ref, o_ref = refs
    tm = z_ref.shape[0]
    z = z_ref[...].astype(F32)
    z = 0.5 * z * (1.0 + jnp.tanh(0.7978845608028654 * (z + 0.044715 * (z * z * z))))
    u = z[:, :CMLP_WIDTH]
    v = z[:, CMLP_WIDTH:]
    xc = v - jnp.mean(v, axis=-1, keepdims=True)
    var = jnp.mean(xc * xc, axis=-1, keepdims=True)
    vn = xc * lax.rsqrt(var + EPS) * lg_ref[...] + lb_ref[...]
    if emit_v:
        v_ref[...] = vn
    vb = vn.astype(BF16)
    for ch in range(tm // n):
        rs = slice(ch * n, (ch + 1) * n)
        for g in range(CMLP_GROUPS):
            cs = slice(g * CMLP_GW, (g + 1) * CMLP_GW)
            mix = _dot(w_ref[g], vb[rs, cs]) + bs_ref[:, cs]
            o_ref[rs, cs] = (u[rs, cs] * mix).astype(o_ref.dtype)


def _cmlp_call(zc, lg, lb, w, bs, tm, n, emit_v):
    ntok = zc.shape[0]
    outs = [jax.ShapeDtypeStruct((ntok, CMLP_WIDTH), BF16)]
    if emit_v:
        outs.append(jax.ShapeDtypeStruct((ntok, CMLP_WIDTH), F32))
    res = pl.pallas_call(
        functools.partial(_cmlp_kernel, n=n, emit_v=emit_v),
        grid=(ntok // tm,),
        in_specs=[
            pl.BlockSpec((tm, 2 * CMLP_WIDTH), lambda i: (i, 0)),
            _const_spec((1, CMLP_WIDTH)), _const_spec((1, CMLP_WIDTH)),
            _const_spec(w.shape), _const_spec(bs.shape),
        ],
        out_specs=[pl.BlockSpec((tm, CMLP_WIDTH), lambda i: (i, 0)) for _ in outs],
        out_shape=outs,
        compiler_params=_cparams(("arbitrary",)),
        name="cmlp",
    )(zc, lg, lb, w, bs)
    return res if emit_v else (res[0], None)


def _merge_kernel(x_ref, sh1_ref, sc1_ref, gt1_ref, sh2_ref, sc2_ref, g1_ref, g2_ref,
                  oa_ref, ob_ref, oc_ref, wmg_ref, bmg_ref, wa_ref, wb_ref, wc_ref, wo_ref,
                  wrh_ref, wrl_ref, br_ref, xo_ref, h2_ref, lg_ref):
    x = x_ref[...]
    h = _modulate(x, g1_ref[...], sh1_ref[...], sc1_ref[...]).astype(BF16)
    d = x.shape[1]
    m = None
    for i, (o_ref, w_ref) in enumerate(((oa_ref, wa_ref), (ob_ref, wb_ref), (oc_ref, wc_ref))):
        gate = _sigmoid(_dot(h, wmg_ref[:, i * d:(i + 1) * d]) + bmg_ref[:, i * d:(i + 1) * d])
        term = gate * _dot(o_ref[...], w_ref[...])
        m = term if m is None else m + term
    xn = x + gt1_ref[...] * _dot(m.astype(BF16), wo_ref[...])
    xo_ref[...] = xn
    h2 = _modulate(xn, g2_ref[...], sh2_ref[...], sc2_ref[...])
    hi = h2.astype(BF16)
    h2_ref[...] = hi
    lo = (h2 - hi.astype(F32)).astype(BF16)
    lg_ref[...] = (_dot(hi, wrh_ref[...]) + _dot(lo, wrh_ref[...]) + _dot(hi, wrl_ref[...])) + br_ref[...]


def _merge_call(x, mods, g1, g2, oa, ob, oc, wmg, bmg, wa, wb, wc, wo, wrh, wrl, br, tm, per_seq_tiles):
    n, d = x.shape
    tok = lambda w: pl.BlockSpec((tm, w), lambda i: (i, 0))
    in_specs = [tok(d)] + [_row_spec(tm, d, per_seq_tiles) for _ in mods]
    in_specs += [_const_spec((1, d)), _const_spec((1, d)), tok(d), tok(d), tok(d)]
    in_specs += [_const_spec(a.shape) for a in (wmg, bmg, wa, wb, wc, wo, wrh, wrl, br)]
    return pl.pallas_call(
        _merge_kernel,
        grid=(n // tm,),
        in_specs=in_specs,
        out_specs=[tok(d), tok(d), tok(ROUTE_W)],
        out_shape=(
            jax.ShapeDtypeStruct((n, d), F32),
            jax.ShapeDtypeStruct((n, d), BF16),
            jax.ShapeDtypeStruct((n, ROUTE_W), F32),
        ),
        compiler_params=_cparams(("arbitrary",)),
        name="merge",
    )(x, *mods, g1, g2, oa, ob, oc, wmg, bmg, wa, wb, wc, wo, wrh, wrl, br)


def _route_kernel(lg_ref, pos_ref, w_ref, meta_ref):
    lg = lg_ref[...]
    tm = lg.shape[0]
    lane = lax.broadcasted_iota(jnp.int32, lg.shape, 1)
    big = jnp.int32(1 << 20)
    ninf = -jnp.inf
    is_g = (lane >= N_EXPERTS) & (lane < N_EXPERTS + MOE_GROUPS)
    gl = jnp.where(is_g, lg, ninf)
    gmax = jnp.max(gl, axis=-1, keepdims=True)
    g_idx = jnp.min(jnp.where(gl == gmax, lane - N_EXPERTS, big), axis=-1, keepdims=True)
    g_val = 1.0 / jnp.sum(jnp.exp(gl - gmax), axis=-1, keepdims=True)
    in_grp = (lane < N_EXPERTS) & ((lane // MOE_PER_GROUP) == g_idx)
    el = jnp.where(in_grp, lg, ninf)
    v0 = jnp.max(el, axis=-1, keepdims=True)
    i0 = jnp.min(jnp.where(el == v0, lane, big), axis=-1, keepdims=True)
    el1 = jnp.where(lane == i0, ninf, el)
    v1 = jnp.max(el1, axis=-1, keepdims=True)
    i1 = jnp.min(jnp.where(el1 == v1, lane, big), axis=-1, keepdims=True)
    e10 = jnp.exp(v1 - v0)
    w0 = g_val / (1.0 + e10)
    w1 = g_val * e10 / (1.0 + e10)

    hit0 = lane == i0
    hit1 = lane == i1
    assign = (hit0 | hit1)
    af = assign.astype(F32)
    r_i = lax.broadcasted_iota(jnp.int32, (tm, tm), 0)
    c_i = lax.broadcasted_iota(jnp.int32, (tm, tm), 1)
    before = (c_i < r_i).astype(BF16)
    rank = _dot(before, assign.astype(BF16))
    cnt = jnp.sum(af, axis=0, keepdims=True)
    cnt8 = jnp.broadcast_to(cnt, (SUBLANES, lg.shape[1]))
    inc = cnt8
    lane8 = lax.broadcasted_iota(jnp.int32, inc.shape, 1)
    sh = 1
    while sh < N_EXPERTS:
        inc = inc + jnp.where(lane8 >= sh, pltpu.roll(inc, sh, 1), 0.0)
        sh *= 2
    off8 = inc - cnt8
    slot = rank + off8[0:1, :]
    p0 = jnp.sum(jnp.where(hit0, slot, 0.0), axis=-1, keepdims=True)
    p1 = jnp.sum(jnp.where(hit1, slot, 0.0), axis=-1, keepdims=True)
    two = lax.broadcasted_iota(jnp.int32, (tm, 2), 1)
    pos_ref[...] = jnp.where(two == 0, p0, p1).astype(jnp.int32)
    w_ref[...] = jnp.where(two == 0, w0, w1)
    row8 = lax.broadcasted_iota(jnp.int32, inc.shape, 0)
    meta_ref[0] = jnp.where(row8 == 0, off8, cnt8).astype(jnp.int32)


def _route_call(lg, tm):
    n = lg.shape[0]
    nt = n // tm
    return pl.pallas_call(
        _route_kernel,
        grid=(nt,),
        in_specs=[pl.BlockSpec((tm, ROUTE_W), lambda i: (i, 0))],
        out_specs=[
            pl.BlockSpec((tm, 2), lambda i: (i, 0)),
            pl.BlockSpec((tm, 2), lambda i: (i, 0)),
            pl.BlockSpec((1, SUBLANES, ROUTE_W), lambda i: (i, 0, 0)),
        ],
        out_shape=(
            jax.ShapeDtypeStruct((n, 2), jnp.int32),
            jax.ShapeDtypeStruct((n, 2), F32),
            jax.ShapeDtypeStruct((nt, SUBLANES, ROUTE_W), jnp.int32),
        ),
        compiler_params=_cparams(("arbitrary",)),
        name="route",
    )(lg)


def _moe_kernel(pos_ref, meta_ref, h_ref, x_ref, w2_ref, gt_ref, fg_ref, wg_ref, wu_ref, wd_ref,
                o_ref, xt_ref, slot_ref, *, tm, c, final):
    i = pl.program_id(0)
    e = pl.program_id(1)
    nchunk = D_MODEL // LANES
    nslot = 2 * tm
    pbase = i * nslot

    @pl.when(e == 0)
    def _():
        for s in range(nchunk):
            xt_ref[pl.ds(s, tm, stride=nchunk), :] = h_ref[:, s * LANES:(s + 1) * LANES].astype(F32)
        slot_ref[pl.ds(nslot * nchunk, c * nchunk), :] = jnp.zeros((c * nchunk, LANES), F32)

        def disp(j, carry):
            for u in range(8):
                t = j * 8 + u
                row = xt_ref[pl.ds(pl.multiple_of(t * nchunk, nchunk), nchunk), :]
                for k in range(2):
                    p = pos_ref[pbase + 2 * t + k]
                    slot_ref[pl.ds(pl.multiple_of(p * nchunk, nchunk), nchunk), :] = row
            return carry
        lax.fori_loop(0, tm // 8, disp, 0)

    start = meta_ref[i * 2 * N_EXPERTS + e]
    cnt = meta_ref[i * 2 * N_EXPERTS + N_EXPERTS + e]

    def block(b, carry):
        r0 = pl.multiple_of((start + b * c) * nchunk, nchunk)
        xg = jnp.concatenate(
            [slot_ref[pl.ds(r0 + s, c, stride=nchunk), :] for s in range(nchunk)], axis=1)
        xb = xg.astype(BF16)
        gate = _dot(xb, wg_ref[...])
        hid = (gate * _sigmoid(gate)) * _dot(xb, wu_ref[...])
        yb = _dot(hid.astype(BF16), wd_ref[...])
        keep = lax.broadcasted_iota(jnp.int32, (c, 1), 0) < (cnt - b * c)
        yb = jnp.where(keep, yb, xg)
        for s in range(nchunk):
            slot_ref[pl.ds(r0 + s, c, stride=nchunk), :] = yb[:, s * LANES:(s + 1) * LANES]
        return carry
    lax.fori_loop(0, (cnt + c - 1) // c, block, 0)

    @pl.when(e == pl.num_programs(1) - 1)
    def _():
        acc = None
        for k in range(2):
            def comb(j, carry):
                for u in range(8):
                    t = j * 8 + u
                    p = pos_ref[pbase + 2 * t + k]
                    xt_ref[pl.ds(pl.multiple_of(t * nchunk, nchunk), nchunk), :] = (
                        slot_ref[pl.ds(pl.multiple_of(p * nchunk, nchunk), nchunk), :])
                return carry
            lax.fori_loop(0, tm // 8, comb, 0)
            yk = jnp.concatenate(
                [xt_ref[pl.ds(s, tm, stride=nchunk), :] for s in range(nchunk)], axis=1)
            term = yk * w2_ref[:, k:k + 1]
            acc = term if acc is None else acc + term
        xn = x_ref[...] + gt_ref[...] * acc
        if final:
            xn = xn * lax.rsqrt(jnp.mean(xn * xn, axis=-1, keepdims=True) + EPS) * fg_ref[...]
        o_ref[...] = xn


def _moe_call(pos, meta, h2, x, w2, gt, fg, wg, wu, wd, tm, c, per_seq_tiles, final):
    n, d = x.shape
    nt = n // tm
    if per_seq_tiles is None:
        gt_spec = pl.BlockSpec((tm, d), lambda i, e, *_: (i, 0))
    else:
        gt_spec = pl.BlockSpec((None, 1, d), lambda i, e, *_: (i // per_seq_tiles, 0, 0))
    grid_spec = pltpu.PrefetchScalarGridSpec(
        num_scalar_prefetch=2,
        grid=(nt, N_EXPERTS),
        in_specs=[
            pl.BlockSpec((tm, d), lambda i, e, *_: (i, 0)),
            pl.BlockSpec((tm, d), lambda i, e, *_: (i, 0)),
            pl.BlockSpec((tm, 2), lambda i, e, *_: (i, 0)),
            gt_spec,
            pl.BlockSpec((1, d), lambda i, e, *_: (0, 0)),
            pl.BlockSpec((None, d, MOE_HIDDEN), lambda i, e, *_: (e, 0, 0)),
            pl.BlockSpec((None, d, MOE_HIDDEN), lambda i, e, *_: (e, 0, 0)),
            pl.BlockSpec((None, MOE_HIDDEN, d), lambda i, e, *_: (e, 0, 0)),
        ],
        out_specs=pl.BlockSpec((tm, d), lambda i, e, *_: (i, 0)),
        scratch_shapes=[
            pltpu.VMEM((tm * SUBLANES, LANES), F32),
            pltpu.VMEM(((2 * tm + c) * SUBLANES, LANES), F32),
        ],
    )
    return pl.pallas_call(
        functools.partial(_moe_kernel, tm=tm, c=c, final=final),
        grid_spec=grid_spec,
        out_shape=jax.ShapeDtypeStruct((n, d), F32),
        compiler_params=_cparams(("arbitrary", "arbitrary")),
        name="moe",
    )(pos, meta, h2, x, w2, gt, fg, wg, wu, wd)


def _rel_bucket(rel):
    nb = REL_BUCKETS // 2
    max_exact = nb // 2
    ret = jnp.where(rel > 0, nb, 0)
    nabs = jnp.abs(rel)
    nf = jnp.maximum(nabs, 1).astype(F32)
    large = max_exact + (jnp.log(nf / max_exact) / jnp.log(REL_MAX_DIST / max_exact)
                         * (nb - max_exact)).astype(jnp.int32)
    large = jnp.minimum(large, nb - 1)
    return ret + jnp.where(nabs < max_exact, nabs, large)


def _swa_bias(table):
    lwin = WINDOW + CHUNK
    rel = jnp.arange(lwin)[None, :] - WINDOW - jnp.arange(CHUNK)[:, None]
    b = jnp.transpose(table[_rel_bucket(rel)].astype(F32), (2, 0, 1))
    return b.reshape(SWA_KV_HEADS, SWA_GROUP * CHUNK, lwin)


def _layer_params(l, p):
    d = D_MODEL
    w_in = p['w_in'][l]
    o = 0
    cols = {}
    for name, width in (('qa', 512), ('ka', 512), ('va', 1024), ('ga', GLA_RANK), ('ra', 1024),
                        ('qb', 1024), ('kb', 256), ('vb', 256), ('zc', 2048)):
        cols[name] = w_in[:, o:o + width]
        o += width
    lp = {}
    lp['wa'] = jnp.concatenate([cols['qa'], cols['ka'], cols['va'], cols['ra']], axis=1).astype(BF16)
    lp['wq'] = cols['qb'].astype(BF16)
    lp['wkv'] = jnp.concatenate([cols['kb'], cols['vb']], axis=1).astype(BF16)
    lp['wz'] = cols['zc'].astype(BF16)
    lp['wg'] = jnp.pad(cols['ga'], ((0, 0), (0, LANES - GLA_RANK))).astype(BF16)
    lp['gla_wg'] = jnp.pad(p['gla_w_gate'][l], ((0, LANES - GLA_RANK), (0, 0)))
    lp['gla_bg'] = p['gla_b_gate'][l][None, :]
    lp['gla_ng'] = p['gla_norm_g'][l][None, :]
    sink = p['swa_sink'][l].astype(F32).reshape(SWA_KV_HEADS, SWA_GROUP, 1)
    lp['sink'] = jnp.broadcast_to(sink[:, :, None, :], (SWA_KV_HEADS, SWA_GROUP, CHUNK, 1)).reshape(
        SWA_KV_HEADS, SWA_GROUP * CHUNK, 1)
    lp['ln_g'] = p['cmlp_ln_g'][l][None, :]
    lp['ln_b'] = p['cmlp_ln_b'][l][None, :]
    ws = p['cmlp_ws'][l]
    pos = jnp.arange(CMLP_CHUNK)
    mask = (pos[None, :] // CHUNK) <= (pos[:, None] // CHUNK)
    lp['cw_full'] = jnp.where(mask[None], ws, 0.0).astype(BF16)
    lp['cw_half'] = ws[:, :CHUNK, :CHUNK].astype(BF16)
    bs = p['cmlp_bs'][l]
    bs_full = jnp.repeat(bs.T, CMLP_GW, axis=1).astype(F32)
    lp['cb_full'] = bs_full
    lp['cb_half'] = bs_full[:CHUNK]
    lp['wmg'] = p['w_merge_gate'][l].astype(BF16)
    lp['bmg'] = p['b_merge_gate'][l][None, :]
    lp['wba'] = p['w_branch_a'][l].astype(BF16)
    lp['wbb'] = p['w_branch_b'][l].astype(BF16)
    lp['wbc'] = p['w_branch_c'][l].astype(BF16)
    lp['wo'] = p['w_out'][l].astype(BF16)
    wr = jnp.concatenate([p['moe_w_re'][l], p['moe_w_rg'][l]], axis=1)
    wr = jnp.pad(wr, ((0, 0), (0, ROUTE_W - wr.shape[1])))
    wrh = wr.astype(BF16)
    lp['wrh'] = wrh
    lp['wrl'] = (wr - wrh.astype(F32)).astype(BF16)
    br = jnp.concatenate([p['moe_b_re'][l], p['moe_b_rg'][l]])
    lp['br'] = jnp.pad(br, (0, ROUTE_W - br.shape[0]))[None, :].astype(F32)
    lp['ewg'] = p['moe_w_gate'][l].astype(BF16)
    lp['ewu'] = p['moe_w_up'][l].astype(BF16)
    lp['ewd'] = p['moe_w_down'][l].astype(BF16)
    lp['n1'] = p['norm1_g'][l][None, :]
    lp['n2'] = p['norm2_g'][l][None, :]
    return lp


def _pick_tile(n, pref):
    t = min(n, pref)
    while n % t:
        t //= 2
    return t


def _run_group(x3, mod, lps, bias, final_g, prompt, state_gla, cache_k, cache_v):
    nseq, seqlen, d = x3.shape
    n = nseq * seqlen
    x = x3.reshape(n, d)
    depth = len(lps)
    tm = _pick_tile(n, TM_PROJ if prompt else TM_PROJ // 2)
    tm_moe = _pick_tile(n, TM_MOE)
    kvw = SWA_KV_HEADS * SWA_HD
    gla_out, k_out, v_out, vc_out = [], [], [], []
    for l, lp in enumerate(lps):
        m6 = mod[l].reshape(nseq, 6, d)
        if prompt:
            mods = [m6[:, j][:, None, :] for j in range(6)]
            pst = seqlen // tm
            pst_moe = seqlen // tm_moe
        else:
            mods = [jnp.broadcast_to(m6[:, j][:, None, :], (nseq, seqlen, d)).reshape(n, d) for j in range(6)]
            pst = None
            pst_moe = None
        sh1, sc1, gt1, sh2, sc2, gt2 = mods
        pa, qb, kv, zc, ga = _inproj_call(x, sh1, sc1, lp['n1'], lp['wa'], lp['wq'], lp['wkv'],
                                          lp['wz'], lp['wg'], tm, pst)
        tb = _pick_tile(seqlen, TB_GLA)
        s0 = None if prompt else state_gla[l]
        oa, s_new = _gla_call(pa, ga, lp['gla_wg'], lp['gla_bg'], lp['gla_ng'], s0, nseq, seqlen, tb)
        if prompt:
            tq = _pick_tile(seqlen, TQ_SWA)
            ob = _swa_call(qb, (kv, 0), (kv, 1), (kv, 0), (kv, 1), bias, lp['sink'], nseq, seqlen, tq, True)
            kv3 = kv.reshape(nseq, seqlen, 2, SWA_KV_HEADS, SWA_HD)
            k_state = kv3[:, -WINDOW:, 0]
            v_state = kv3[:, -WINDOW:, 1]
        else:
            lc = cache_k.shape[2]
            ck = cache_k[l].reshape(nseq, lc, kvw)
            cv = cache_v[l].reshape(nseq, lc, kvw)
            ob = _swa_call(qb, (kv, 0), (kv, 1), (ck, 0), (cv, 0), bias, lp['sink'], nseq, seqlen, seqlen, False)
            kv3 = kv.reshape(nseq, seqlen, 2, SWA_KV_HEADS, SWA_HD)
            k_state = jnp.concatenate([cache_k[l], kv3[:, :, 0]], axis=1)[:, -lc:]
            v_state = jnp.concatenate([cache_v[l], kv3[:, :, 1]], axis=1)[:, -lc:]
        if prompt:
            oc, _ = _cmlp_call(zc, lp['ln_g'], lp['ln_b'], lp['cw_full'], lp['cb_full'],
                               _pick_tile(n, TM_CMLP), CMLP_CHUNK, False)
        else:
            oc, vc = _cmlp_call(zc, lp['ln_g'], lp['ln_b'], lp['cw_half'], lp['cb_half'],
                                _pick_tile(n, TM_CMLP), seqlen, True)
            vc_out.append(vc.reshape(nseq, seqlen, CMLP_WIDTH))
        x1, h2, lg = _merge_call(x, [sh1, sc1, gt1, sh2, sc2], lp['n1'], lp['n2'], oa, ob, oc,
                                 lp['wmg'], lp['bmg'], lp['wba'], lp['wbb'], lp['wbc'], lp['wo'],
                                 lp['wrh'], lp['wrl'], lp['br'], tm, pst)
        pos, w2, meta = _route_call(lg, tm_moe)
        meta_flat = jnp.concatenate([meta[:, 0, :N_EXPERTS], meta[:, 1, :N_EXPERTS]], axis=1).reshape(-1)
        x = _moe_call(pos.reshape(-1), meta_flat, h2, x1, w2, gt2, final_g, lp['ewg'], lp['ewu'], lp['ewd'],
                      tm_moe, C_MOE, pst_moe, l == depth - 1)
        gla_out.append(s_new)
        k_out.append(k_state)
        v_out.append(v_state)
    y = x.reshape(nseq, seqlen, d)
    vc_stack = jnp.stack(vc_out) if vc_out else None
    return y, jnp.stack(gla_out), jnp.stack(k_out), jnp.stack(v_out), vc_stack


def kernel(x_prompt, x_sample, state_gla, cache_swa_k, cache_swa_v, c_prompt, c_sample, rel_table, final_norm_g, norm1_g, norm2_g, w_mod, b_mod, w_in, gla_w_gate, gla_b_gate, gla_norm_g, swa_sink, cmlp_ln_g, cmlp_ln_b, cmlp_ws, cmlp_bs, w_branch_a, w_branch_b, w_branch_c, w_merge_gate, b_merge_gate, w_out, moe_w_rg, moe_b_rg, moe_w_re, moe_b_re, moe_w_gate, moe_w_up, moe_w_down):
    p = dict(norm1_g=norm1_g, norm2_g=norm2_g, w_in=w_in, gla_w_gate=gla_w_gate, gla_b_gate=gla_b_gate,
             gla_norm_g=gla_norm_g, swa_sink=swa_sink, cmlp_ln_g=cmlp_ln_g, cmlp_ln_b=cmlp_ln_b,
             cmlp_ws=cmlp_ws, cmlp_bs=cmlp_bs, w_branch_a=w_branch_a, w_branch_b=w_branch_b,
             w_branch_c=w_branch_c, w_merge_gate=w_merge_gate, b_merge_gate=b_merge_gate, w_out=w_out,
             moe_w_rg=moe_w_rg, moe_b_rg=moe_b_rg, moe_w_re=moe_w_re, moe_b_re=moe_b_re,
             moe_w_gate=moe_w_gate, moe_w_up=moe_w_up, moe_w_down=moe_w_down)
    depth = w_in.shape[0]
    nb = c_prompt.shape[0]
    lps = [_layer_params(l, p) for l in range(depth)]
    bias = _swa_bias(rel_table)
    mod = _mod_call(jnp.concatenate([c_prompt, c_sample], axis=0), w_mod, b_mod)
    fg = final_norm_g[None, :]
    y_p, gla_p, k_p, v_p, _ = _run_group(x_prompt, mod[:, :nb], lps, bias, fg, True, None, None, None)
    y_s, gla_s, k_s, v_s, vc_s = _run_group(x_sample, mod[:, nb:], lps, bias, fg, False,
                                            state_gla, cache_swa_k, cache_swa_v)
    return (y_p, y_s, gla_p, k_p, v_p, gla_s, k_s, v_s, vc_s)
```

```python
import functools

import jax
import jax.numpy as jnp
from jax import lax
from jax.experimental import pallas as pl
from jax.experimental.pallas import tpu as pltpu

F32 = jnp.float32
BF16 = jnp.bfloat16
HIGHEST = lax.Precision.HIGHEST

D_MODEL = 1024
CHUNK = 64
EPS = 1e-6
GLA_HEADS = 4
GLA_DK = 128
GLA_DV = 256
GLA_RANK = 16
GLA_TAU = 16.0
SWA_HEADS = 16
SWA_KV_HEADS = 4
SWA_GROUP = 4
SWA_HD = 64
WINDOW = 128
REL_BUCKETS = 32
REL_MAX_DIST = 128
CMLP_CHUNK = 128
CMLP_GROUPS = 4
CMLP_WIDTH = 1024
CMLP_GW = 256
MOE_GROUPS = 4
MOE_PER_GROUP = 8
N_EXPERTS = 32
MOE_HIDDEN = 256

LANES = 128
SUBLANES = 8
VMEM_LIMIT = 56 * 1024 * 1024

TM_PROJ = 512
TB_GLA = 512
TQ_SWA = 256
TM_CMLP = 256
TM_MOE = 1024
C_MOE = 128
EPS_MOE = 4
ROUTE_W = 128


def _cparams(sem):
    return pltpu.CompilerParams(dimension_semantics=sem, vmem_limit_bytes=VMEM_LIMIT)


def _dot(a, b):
    return jnp.dot(a, b, preferred_element_type=F32)


def _dot_nt(a, b):
    return lax.dot_general(a, b, (((1,), (1,)), ((), ())), preferred_element_type=F32)


def _dot_tn(a, b):
    return lax.dot_general(a, b, (((0,), (0,)), ((), ())), preferred_element_type=F32)


def _sigmoid(x):
    return 1.0 / (1.0 + jnp.exp(-x))


def _modulate(x, g, shift, scale):
    ms = jnp.mean(x * x, axis=-1, keepdims=True)
    return x * lax.rsqrt(ms + EPS) * g * (1.0 + scale) + shift


def _mod_kernel(c_ref, w_ref, b_ref, o_ref):
    c = c_ref[...]
    s = c * _sigmoid(c)
    o_ref[0] = jnp.dot(s, w_ref[0], precision=HIGHEST, preferred_element_type=F32) + b_ref[0]


def _mod_call(c_all, w_mod, b_mod):
    depth, d, n = w_mod.shape
    r = c_all.shape[0]
    bn = 1024
    return pl.pallas_call(
        _mod_kernel,
        grid=(depth, n // bn),
        in_specs=[
            pl.BlockSpec((r, d), lambda l, j: (0, 0)),
            pl.BlockSpec((1, d, bn), lambda l, j: (l, 0, j)),
            pl.BlockSpec((1, 1, bn), lambda l, j: (l, 0, j)),
        ],
        out_specs=pl.BlockSpec((1, r, bn), lambda l, j: (l, 0, j)),
        out_shape=jax.ShapeDtypeStruct((depth, r, n), F32),
        compiler_params=_cparams(("arbitrary", "arbitrary")),
        name="mod",
    )(c_all, w_mod, b_mod.reshape(depth, 1, n))


def _inproj_kernel(x_ref, sh_ref, sc_ref, g_ref, wa_ref, wq_ref, wkv_ref, wz_ref, wg_ref,
                   pa_ref, qb_ref, kv_ref, zc_ref, ga_ref):
    h = _modulate(x_ref[...], g_ref[...], sh_ref[...], sc_ref[...]).astype(BF16)
    col = 512
    for w_ref, o_ref in ((wa_ref, pa_ref), (wq_ref, qb_ref), (wkv_ref, kv_ref), (wz_ref, zc_ref)):
        n = w_ref.shape[1]
        for j in range(0, n, col):
            o_ref[:, j:j + col] = _dot(h, w_ref[:, j:j + col]).astype(o_ref.dtype)
    ga_ref[...] = _dot(h, wg_ref[...])


def _row_spec(tm, width, per_seq_tiles):
    if per_seq_tiles is None:
        return pl.BlockSpec((tm, width), lambda i: (i, 0))
    return pl.BlockSpec((None, 1, width), lambda i: (i // per_seq_tiles, 0, 0))


def _const_spec(shape):
    nd = len(shape)
    return pl.BlockSpec(shape, lambda i: (0,) * nd, pipeline_mode=pl.Buffered(1))


def _inproj_call(x, sh, sc, g, wa, wq, wkv, wz, wg, tm, per_seq_tiles):
    n, d = x.shape
    outs = (
        jax.ShapeDtypeStruct((n, wa.shape[1]), BF16),
        jax.ShapeDtypeStruct((n, wq.shape[1]), BF16),
        jax.ShapeDtypeStruct((n, wkv.shape[1]), F32),
        jax.ShapeDtypeStruct((n, wz.shape[1]), BF16),
        jax.ShapeDtypeStruct((n, LANES), F32),
    )
    return pl.pallas_call(
        _inproj_kernel,
        grid=(n // tm,),
        in_specs=[
            pl.BlockSpec((tm, d), lambda i: (i, 0)),
            _row_spec(tm, d, per_seq_tiles),
            _row_spec(tm, d, per_seq_tiles),
            _const_spec((1, d)),
            _const_spec(wa.shape), _const_spec(wq.shape), _const_spec(wkv.shape),
            _const_spec(wz.shape), _const_spec(wg.shape),
        ],
        out_specs=[pl.BlockSpec((tm, o.shape[1]), lambda i: (i, 0)) for o in outs],
        out_shape=outs,
        compiler_params=_cparams(("arbitrary",)),
        name="inproj",
    )(x, sh, sc, g, wa, wq, wkv, wz, wg)


def _gla_kernel(*refs, nblk, has_state):
    if has_state:
        (q_ref, k_ref, v_ref, r_ref, ga_ref, wg_ref, bg_ref, ng_ref, s0_ref,
         o_ref, sfin_ref, st_ref) = refs
    else:
        (q_ref, k_ref, v_ref, r_ref, ga_ref, wg_ref, bg_ref, ng_ref,
         o_ref, sfin_ref, st_ref) = refs
        s0_ref = None
    t = pl.program_id(2)

    @pl.when(t == 0)
    def _():
        if has_state:
            st_ref[...] = s0_ref[0, 0].T
        else:
            st_ref[...] = jnp.zeros_like(st_ref)

    row = lax.broadcasted_iota(jnp.int32, (CHUNK, CHUNK), 0)
    colm = lax.broadcasted_iota(jnp.int32, (CHUNK, CHUNK), 1)
    causal = row >= colm
    tril = causal.astype(F32)
    gl = _dot(ga_ref[...].astype(BF16), wg_ref[...].astype(BF16)) + bg_ref[...]
    la = (jnp.minimum(gl, 0.0) - jnp.log(1.0 + jnp.exp(-jnp.abs(gl)))) / GLA_TAU
    sls = [slice(c * CHUNK, (c + 1) * CHUNK) for c in range(nblk)]
    cum = [jnp.dot(tril, la[sl], precision=HIGHEST, preferred_element_type=F32) for sl in sls]
    qts, kts, kds, decs = [], [], [], []
    for c, sl in enumerate(sls):
        b = cum[c]
        bl = b[CHUNK - 1:CHUNK, :]
        q = q_ref[sl, :].astype(F32) * (GLA_DK ** -0.5)
        k = k_ref[sl, :].astype(F32)
        qts.append((q * jnp.exp(b)).astype(BF16))
        kts.append((k * jnp.exp(-b)).astype(BF16))
        kds.append((k * jnp.exp(bl - b)).astype(BF16))
        decs.append(jnp.exp(bl))
    amats = [jnp.where(causal, _dot_nt(qts[c], kts[c]), 0.0).astype(BF16) for c in range(nblk)]
    kvs = [_dot_tn(v_ref[sls[c], :], kds[c]) for c in range(nblk)]
    st = st_ref[...]
    states = []
    for c in range(nblk):
        states.append(st.astype(BF16))
        st = st * decs[c] + kvs[c]
    st_ref[...] = st
    for c, sl in enumerate(sls):
        o = _dot(amats[c], v_ref[sl, :]) + _dot_nt(qts[c], states[c])
        on = o * lax.rsqrt(jnp.mean(o * o, axis=-1, keepdims=True) + EPS) * ng_ref[...]
        r = r_ref[sl, :].astype(F32)
        o_ref[sl, :] = (on * (r * _sigmoid(r))).astype(o_ref.dtype)

    @pl.when(t == pl.num_programs(2) - 1)
    def _():
        sfin_ref[0, 0] = st_ref[...].T


def _gla_call(pa, ga, wg, bg, ng, s0, nseq, seqlen, tb):
    n = pa.shape[0]
    nt = seqlen // tb
    kq = GLA_HEADS
    vq = (2 * GLA_HEADS * GLA_DK) // GLA_DV
    rq = vq + GLA_HEADS
    in_specs = [
        pl.BlockSpec((tb, GLA_DK), lambda b, h, t: (b * nt + t, h)),
        pl.BlockSpec((tb, GLA_DK), lambda b, h, t: (b * nt + t, kq + h)),
        pl.BlockSpec((tb, GLA_DV), lambda b, h, t: (b * nt + t, vq + h)),
        pl.BlockSpec((tb, GLA_DV), lambda b, h, t: (b * nt + t, rq + h)),
        pl.BlockSpec((tb, LANES), lambda b, h, t: (b * nt + t, 0)),
        pl.BlockSpec((LANES, GLA_DK), lambda b, h, t: (0, h)),
        pl.BlockSpec((1, GLA_DK), lambda b, h, t: (0, h)),
        pl.BlockSpec((1, GLA_DV), lambda b, h, t: (0, 0)),
    ]
    args = [pa, pa, pa, pa, ga, wg, bg, ng]
    if s0 is not None:
        in_specs.append(pl.BlockSpec((1, 1, GLA_DK, GLA_DV), lambda b, h, t: (b, h, 0, 0)))
        args.append(s0)
    return pl.pallas_call(
        functools.partial(_gla_kernel, nblk=tb // CHUNK, has_state=s0 is not None),
        grid=(nseq, GLA_HEADS, nt),
        in_specs=in_specs,
        out_specs=[
            pl.BlockSpec((tb, GLA_DV), lambda b, h, t: (b * nt + t, h)),
            pl.BlockSpec((1, 1, GLA_DK, GLA_DV), lambda b, h, t: (b, h, 0, 0)),
        ],
        out_shape=(
            jax.ShapeDtypeStruct((n, GLA_HEADS * GLA_DV), BF16),
            jax.ShapeDtypeStruct((nseq, GLA_HEADS, GLA_DK, GLA_DV), F32),
        ),
        scratch_shapes=[pltpu.VMEM((GLA_DV, GLA_DK), F32)],
        compiler_params=_cparams(("arbitrary", "arbitrary", "arbitrary")),
        name="gla",
    )(*args)


def _swa_kernel(q_ref, kc_ref, vc_ref, kp_ref, vp_ref, bias_ref, sink_ref, o_ref, *, tq, mask_start):
    blk = pl.program_id(1)
    kall = jnp.concatenate([kp_ref[...], kc_ref[...]], axis=0).astype(BF16)
    vall = jnp.concatenate([vp_ref[...], vc_ref[...]], axis=0).astype(BF16)
    lwin = WINDOW + CHUNK
    cols = SWA_GROUP * CHUNK
    for c in range(tq // CHUNK):
        need_mask = mask_start and c * CHUNK < WINDOW
        if need_mask:
            first_pos = blk * tq + c * CHUNK - WINDOW
            valid = lax.broadcasted_iota(jnp.int32, (lwin, cols), 0) + first_pos >= 0
        scores = []
        for x in range(SWA_KV_HEADS):
            kw = kall[c * CHUNK:c * CHUNK + lwin, x * SWA_HD:(x + 1) * SWA_HD]
            base = x * SWA_GROUP * SWA_HD
            qs = jnp.concatenate(
                [q_ref[c * CHUNK:(c + 1) * CHUNK, base + g * SWA_HD:base + (g + 1) * SWA_HD]
                 for g in range(SWA_GROUP)], axis=0)
            scores.append(_dot_nt(kw, qs))
        probs = []
        for x in range(SWA_KV_HEADS):
            s = scores[x] + bias_ref[x]
            if need_mask:
                s = jnp.where(valid, s, -1e30)
            sink = sink_ref[x]
            m = jnp.maximum(jnp.max(s, axis=0, keepdims=True), sink)
            p = jnp.exp(s - m)
            den = jnp.sum(p, axis=0, keepdims=True) + jnp.exp(sink - m)
            probs.append((p.astype(BF16), 1.0 / den))
        for x in range(SWA_KV_HEADS):
            vw = vall[c * CHUNK:c * CHUNK + lwin, x * SWA_HD:(x + 1) * SWA_HD]
            base = x * SWA_GROUP * SWA_HD
            vpad = jnp.concatenate([vw, jnp.zeros_like(vw)], axis=1)
            p, rden = probs[x]
            o = (_dot_tn(vpad, p) * rden).T
            o_ref[c * CHUNK:(c + 1) * CHUNK, base:base + SWA_GROUP * SWA_HD] = jnp.concatenate(
                [o[g * CHUNK:(g + 1) * CHUNK, :SWA_HD] for g in range(SWA_GROUP)], axis=1).astype(o_ref.dtype)


def _swa_call(qb, k_cur, v_cur, k_prev, v_prev, bias, sink, nseq, seqlen, tq, prompt):
    n = qb.shape[0]
    nt = seqlen // tq
    kvw = SWA_KV_HEADS * SWA_HD
    per_prev = tq // WINDOW if prompt else None

    def cur_spec(cb):
        return pl.BlockSpec((tq, kvw), lambda b, t: (b * nt + t, cb))

    def prev_spec(cb):
        if prompt:
            nprev = seqlen // WINDOW
            return pl.BlockSpec((WINDOW, kvw),
                                lambda b, t: (b * nprev + jnp.maximum(t * per_prev - 1, 0), cb))
        return pl.BlockSpec((None, WINDOW, kvw), lambda b, t: (b, 0, cb))

    return pl.pallas_call(
        functools.partial(_swa_kernel, tq=tq, mask_start=prompt),
        grid=(nseq, nt),
        in_specs=[
            pl.BlockSpec((tq, SWA_HEADS * SWA_HD), lambda b, t: (b * nt + t, 0)),
            cur_spec(k_cur[1]), cur_spec(v_cur[1]), prev_spec(k_prev[1]), prev_spec(v_prev[1]),
            pl.BlockSpec(bias.shape, lambda b, t: (0, 0, 0)),
            pl.BlockSpec(sink.shape, lambda b, t: (0, 0, 0)),
        ],
        out_specs=pl.BlockSpec((tq, SWA_HEADS * SWA_HD), lambda b, t: (b * nt + t, 0)),
        out_shape=jax.ShapeDtypeStruct((n, SWA_HEADS * SWA_HD), BF16),
        compiler_params=_cparams(("arbitrary", "arbitrary")),
        name="swa",
    )(qb, k_cur[0], v_cur[0], k_prev[0], v_prev[0], bias, sink)


def _cmlp_kernel(*refs, n, emit_v):
    if emit_v:
        z_ref, lg_ref, lb_ref, w_ref, bs_ref, o_ref, v_ref = refs
    else:
        z_ref, lg_ref, lb_ref, w_ref, bs_ref, o_ref = refs
    tm = z_ref.shape[0]
    z = z_ref[...].astype(F32)
    z = 0.5 * z * (1.0 + jnp.tanh(0.7978845608028654 * (z + 0.044715 * (z * z * z))))
    u = z[:, :CMLP_WIDTH]
    v = z[:, CMLP_WIDTH:]
    xc = v - jnp.mean(v, axis=-1, keepdims=True)
    var = jnp.mean(xc * xc, axis=-1, keepdims=True)
    vn = xc * lax.rsqrt(var + EPS) * lg_ref[...] + lb_ref[...]
    if emit_v:
        v_ref[...] = vn
    vb = vn.astype(BF16)
    for ch in range(tm // n):
        rs = slice(ch * n, (ch + 1) * n)
        for g in range(CMLP_GROUPS):
            cs = slice(g * CMLP_GW, (g + 1) * CMLP_GW)
            mix = _dot(w_ref[g], vb[rs, cs]) + bs_ref[:, cs]
            o_ref[rs, cs] = (u[rs, cs] * mix).astype(o_ref.dtype)


def _cmlp_call(zc, lg, lb, w, bs, tm, n, emit_v):
    ntok = zc.shape[0]
    outs = [jax.ShapeDtypeStruct((ntok, CMLP_WIDTH), BF16)]
    if emit_v:
        outs.append(jax.ShapeDtypeStruct((ntok, CMLP_WIDTH), F32))
    res = pl.pallas_call(
        functools.partial(_cmlp_kernel, n=n, emit_v=emit_v),
        grid=(ntok // tm,),
        in_specs=[
            pl.BlockSpec((tm, 2 * CMLP_WIDTH), lambda i: (i, 0)),
            _const_spec((1, CMLP_WIDTH)), _const_spec((1, CMLP_WIDTH)),
            _const_spec(w.shape), _const_spec(bs.shape),
        ],
        out_specs=[pl.BlockSpec((tm, CMLP_WIDTH), lambda i: (i, 0)) for _ in outs],
        out_shape=outs,
        compiler_params=_cparams(("arbitrary",)),
        name="cmlp",
    )(zc, lg, lb, w, bs)
    return res if emit_v else (res[0], None)


def _merge_kernel(x_ref, sh1_ref, sc1_ref, gt1_ref, sh2_ref, sc2_ref, g1_ref, g2_ref,
                  oa_ref, ob_ref, oc_ref, wmg_ref, bmg_ref, wa_ref, wb_ref, wc_ref, wo_ref,
                  wrh_ref, wrl_ref, br_ref, xo_ref, h2_ref, lg_ref):
    x = x_ref[...]
    h = _modulate(x, g1_ref[...], sh1_ref[...], sc1_ref[...]).astype(BF16)
    d = x.shape[1]
    m = None
    for i, (o_ref, w_ref) in enumerate(((oa_ref, wa_ref), (ob_ref, wb_ref), (oc_ref, wc_ref))):
        gate = _sigmoid(_dot(h, wmg_ref[:, i * d:(i + 1) * d]) + bmg_ref[:, i * d:(i + 1) * d])
        term = gate * _dot(o_ref[...], w_ref[...])
        m = term if m is None else m + term
    xn = x + gt1_ref[...] * _dot(m.astype(BF16), wo_ref[...])
    xo_ref[...] = xn
    h2 = _modulate(xn, g2_ref[...], sh2_ref[...], sc2_ref[...])
    hi = h2.astype(BF16)
    h2_ref[...] = hi
    lo = (h2 - hi.astype(F32)).astype(BF16)
    lg_ref[...] = (_dot(hi, wrh_ref[...]) + _dot(lo, wrh_ref[...]) + _dot(hi, wrl_ref[...])) + br_ref[...]


def _merge_call(x, mods, g1, g2, oa, ob, oc, wmg, bmg, wa, wb, wc, wo, wrh, wrl, br, tm, per_seq_tiles):
    n, d = x.shape
    tok = lambda w: pl.BlockSpec((tm, w), lambda i: (i, 0))
    in_specs = [tok(d)] + [_row_spec(tm, d, per_seq_tiles) for _ in mods]
    in_specs += [_const_spec((1, d)), _const_spec((1, d)), tok(d), tok(d), tok(d)]
    in_specs += [_const_spec(a.shape) for a in (wmg, bmg, wa, wb, wc, wo, wrh, wrl, br)]
    return pl.pallas_call(
        _merge_kernel,
        grid=(n // tm,),
        in_specs=in_specs,
        out_specs=[tok(d), tok(d), tok(ROUTE_W)],
        out_shape=(
            jax.ShapeDtypeStruct((n, d), F32),
            jax.ShapeDtypeStruct((n, d), BF16),
            jax.ShapeDtypeStruct((n, ROUTE_W), F32),
        ),
        compiler_params=_cparams(("arbitrary",)),
        name="merge",
    )(x, *mods, g1, g2, oa, ob, oc, wmg, bmg, wa, wb, wc, wo, wrh, wrl, br)


def _route_kernel(lg_ref, pos_ref, w_ref, meta_ref):
    lg = lg_ref[...]
    tm = lg.shape[0]
    lane = lax.broadcasted_iota(jnp.int32, lg.shape, 1)
    big = jnp.int32(1 << 20)
    ninf = -jnp.inf
    is_g = (lane >= N_EXPERTS) & (lane < N_EXPERTS + MOE_GROUPS)
    gl = jnp.where(is_g, lg, ninf)
    gmax = jnp.max(gl, axis=-1, keepdims=True)
    g_idx = jnp.min(jnp.where(gl == gmax, lane - N_EXPERTS, big), axis=-1, keepdims=True)
    g_val = 1.0 / jnp.sum(jnp.exp(gl - gmax), axis=-1, keepdims=True)
    in_grp = (lane < N_EXPERTS) & ((lane // MOE_PER_GROUP) == g_idx)
    el = jnp.where(in_grp, lg, ninf)
    v0 = jnp.max(el, axis=-1, keepdims=True)
    i0 = jnp.min(jnp.where(el == v0, lane, big), axis=-1, keepdims=True)
    el1 = jnp.where(lane == i0, ninf, el)
    v1 = jnp.max(el1, axis=-1, keepdims=True)
    i1 = jnp.min(jnp.where(el1 == v1, lane, big), axis=-1, keepdims=True)
    e10 = jnp.exp(v1 - v0)
    w0 = g_val / (1.0 + e10)
    w1 = g_val * e10 / (1.0 + e10)

    hit0 = lane == i0
    hit1 = lane == i1
    assign = (hit0 | hit1)
    af = assign.astype(F32)
    r_i = lax.broadcasted_iota(jnp.int32, (tm, tm), 0)
    c_i = lax.broadcasted_iota(jnp.int32, (tm, tm), 1)
    before = (c_i < r_i).astype(BF16)
    rank = _dot(before, assign.astype(BF16))
    cnt = jnp.sum(af, axis=0, keepdims=True)
    cnt8 = jnp.broadcast_to(cnt, (SUBLANES, lg.shape[1]))
    inc = cnt8
    lane8 = lax.broadcasted_iota(jnp.int32, inc.shape, 1)
    sh = 1
    while sh < N_EXPERTS:
        inc = inc + jnp.where(lane8 >= sh, pltpu.roll(inc, sh, 1), 0.0)
        sh *= 2
    off8 = inc - cnt8
    slot = rank + off8[0:1, :]
    p0 = jnp.sum(jnp.where(hit0, slot, 0.0), axis=-1, keepdims=True)
    p1 = jnp.sum(jnp.where(hit1, slot, 0.0), axis=-1, keepdims=True)
    two = lax.broadcasted_iota(jnp.int32, (tm, 2), 1)
    pos_ref[...] = jnp.where(two == 0, p0, p1).astype(jnp.int32)
    w_ref[...] = jnp.where(two == 0, w0, w1)
    row8 = lax.broadcasted_iota(jnp.int32, inc.shape, 0)
    meta_ref[0] = jnp.where(row8 == 0, off8, cnt8).astype(jnp.int32)


def _route_call(lg, tm):
    n = lg.shape[0]
    nt = n // tm
    return pl.pallas_call(
        _route_kernel,
        grid=(nt,),
        in_specs=[pl.BlockSpec((tm, ROUTE_W), lambda i: (i, 0))],
        out_specs=[
            pl.BlockSpec((tm, 2), lambda i: (i, 0)),
            pl.BlockSpec((tm, 2), lambda i: (i, 0)),
            pl.BlockSpec((1, SUBLANES, ROUTE_W), lambda i: (i, 0, 0)),
        ],
        out_shape=(
            jax.ShapeDtypeStruct((n, 2), jnp.int32),
            jax.ShapeDtypeStruct((n, 2), F32),
            jax.ShapeDtypeStruct((nt, SUBLANES, ROUTE_W), jnp.int32),
        ),
        compiler_params=_cparams(("arbitrary",)),
        name="route",
    )(lg)


def _moe_kernel(pos_ref, meta_ref, h_ref, x_ref, w2_ref, gt_ref, fg_ref, wg_ref, wu_ref, wd_ref,
                o_ref, xt_ref, slot_ref, *, tm, c, eps, final):
    i = pl.program_id(0)
    e = pl.program_id(1)
    nchunk = D_MODEL // LANES
    nslot = 2 * tm
    pbase = i * nslot

    @pl.when(e == 0)
    def _():
        for s in range(nchunk):
            xt_ref[pl.ds(s, tm, stride=nchunk), :] = h_ref[:, s * LANES:(s + 1) * LANES].astype(F32)
        slot_ref[pl.ds(nslot * nchunk, c * nchunk), :] = jnp.zeros((c * nchunk, LANES), F32)

        def disp(j, carry):
            for u in range(8):
                t = j * 8 + u
                row = xt_ref[pl.ds(pl.multiple_of(t * nchunk, nchunk), nchunk), :]
                for k in range(2):
                    p = pos_ref[pbase + 2 * t + k]
                    slot_ref[pl.ds(pl.multiple_of(p * nchunk, nchunk), nchunk), :] = row
            return carry
        lax.fori_loop(0, tm // 8, disp, 0)

    for j in range(eps):
        ex = e * eps + j
        start = meta_ref[i * 2 * N_EXPERTS + ex]
        cnt = meta_ref[i * 2 * N_EXPERTS + N_EXPERTS + ex]

        def block(b, carry, j=j, start=start, cnt=cnt):
            r0 = pl.multiple_of((start + b * c) * nchunk, nchunk)
            xg = jnp.concatenate(
                [slot_ref[pl.ds(r0 + s, c, stride=nchunk), :] for s in range(nchunk)], axis=1)
            xb = xg.astype(BF16)
            gate = _dot(xb, wg_ref[j])
            hid = (gate * _sigmoid(gate)) * _dot(xb, wu_ref[j])
            yb = _dot(hid.astype(BF16), wd_ref[j])
            keep = lax.broadcasted_iota(jnp.int32, (c, 1), 0) < (cnt - b * c)
            yb = jnp.where(keep, yb, xg)
            for s in range(nchunk):
                slot_ref[pl.ds(r0 + s, c, stride=nchunk), :] = yb[:, s * LANES:(s + 1) * LANES]
            return carry
        lax.fori_loop(0, (cnt + c - 1) // c, block, 0)

    @pl.when(e == pl.num_programs(1) - 1)
    def _():
        acc = None
        for k in range(2):
            def comb(j, carry):
                for u in range(8):
                    t = j * 8 + u
                    p = pos_ref[pbase + 2 * t + k]
                    xt_ref[pl.ds(pl.multiple_of(t * nchunk, nchunk), nchunk), :] = (
                        slot_ref[pl.ds(pl.multiple_of(p * nchunk, nchunk), nchunk), :])
                return carry
            lax.fori_loop(0, tm // 8, comb, 0)
            yk = jnp.concatenate(
                [xt_ref[pl.ds(s, tm, stride=nchunk), :] for s in range(nchunk)], axis=1)
            term = yk * w2_ref[:, k:k + 1]
            acc = term if acc is None else acc + term
        xn = x_ref[...] + gt_ref[...] * acc
        if final:
            xn = xn * lax.rsqrt(jnp.mean(xn * xn, axis=-1, keepdims=True) + EPS) * fg_ref[...]
        o_ref[...] = xn


def _moe_call(pos, meta, h2, x, w2, gt, fg, wg, wu, wd, tm, c, eps, per_seq_tiles, final):
    n, d = x.shape
    nt = n // tm
    if per_seq_tiles is None:
        gt_spec = pl.BlockSpec((tm, d), lambda i, e, *_: (i, 0))
    else:
        gt_spec = pl.BlockSpec((None, 1, d), lambda i, e, *_: (i // per_seq_tiles, 0, 0))
    grid_spec = pltpu.PrefetchScalarGridSpec(
        num_scalar_prefetch=2,
        grid=(nt, N_EXPERTS // eps),
        in_specs=[
            pl.BlockSpec((tm, d), lambda i, e, *_: (i, 0)),
            pl.BlockSpec((tm, d), lambda i, e, *_: (i, 0)),
            pl.BlockSpec((tm, 2), lambda i, e, *_: (i, 0)),
            gt_spec,
            pl.BlockSpec((1, d), lambda i, e, *_: (0, 0)),
            pl.BlockSpec((eps, d, MOE_HIDDEN), lambda i, e, *_: (e, 0, 0)),
            pl.BlockSpec((eps, d, MOE_HIDDEN), lambda i, e, *_: (e, 0, 0)),
            pl.BlockSpec((eps, MOE_HIDDEN, d), lambda i, e, *_: (e, 0, 0)),
        ],
        out_specs=pl.BlockSpec((tm, d), lambda i, e, *_: (i, 0)),
        scratch_shapes=[
            pltpu.VMEM((tm * SUBLANES, LANES), F32),
            pltpu.VMEM(((2 * tm + c) * SUBLANES, LANES), F32),
        ],
    )
    return pl.pallas_call(
        functools.partial(_moe_kernel, tm=tm, c=c, eps=eps, final=final),
        grid_spec=grid_spec,
        out_shape=jax.ShapeDtypeStruct((n, d), F32),
        compiler_params=_cparams(("arbitrary", "arbitrary")),
        name="moe",
    )(pos, meta, h2, x, w2, gt, fg, wg, wu, wd)


def _rel_bucket(rel):
    nb = REL_BUCKETS // 2
    max_exact = nb // 2
    ret = jnp.where(rel > 0, nb, 0)
    nabs = jnp.abs(rel)
    nf = jnp.maximum(nabs, 1).astype(F32)
    large = max_exact + (jnp.log(nf / max_exact) / jnp.log(REL_MAX_DIST / max_exact)
                         * (nb - max_exact)).astype(jnp.int32)
    large = jnp.minimum(large, nb - 1)
    return ret + jnp.where(nabs < max_exact, nabs, large)


def _swa_bias(table):
    lwin = WINDOW + CHUNK
    rel = jnp.arange(lwin)[None, :] - WINDOW - jnp.arange(CHUNK)[:, None]
    b = jnp.transpose(table[_rel_bucket(rel)].astype(F32), (2, 0, 1))
    b = b.reshape(SWA_KV_HEADS, SWA_GROUP * CHUNK, lwin)
    return jnp.transpose(b, (0, 2, 1))


def _layer_params(l, p):
    d = D_MODEL
    w_in = p['w_in'][l]
    o = 0
    cols = {}
    for name, width in (('qa', 512), ('ka', 512), ('va', 1024), ('ga', GLA_RANK), ('ra', 1024),
                        ('qb', 1024), ('kb', 256), ('vb', 256), ('zc', 2048)):
        cols[name] = w_in[:, o:o + width]
        o += width
    lp = {}
    lp['wa'] = jnp.concatenate([cols['qa'], cols['ka'], cols['va'], cols['ra']], axis=1).astype(BF16)
    lp['wq'] = (cols['qb'] * (SWA_HD ** -0.5)).astype(BF16)
    lp['wkv'] = jnp.concatenate([cols['kb'], cols['vb']], axis=1).astype(BF16)
    lp['wz'] = cols['zc'].astype(BF16)
    lp['wg'] = jnp.pad(cols['ga'], ((0, 0), (0, LANES - GLA_RANK))).astype(BF16)
    lp['gla_wg'] = jnp.pad(p['gla_w_gate'][l], ((0, LANES - GLA_RANK), (0, 0)))
    lp['gla_bg'] = p['gla_b_gate'][l][None, :]
    lp['gla_ng'] = p['gla_norm_g'][l][None, :]
    sink = p['swa_sink'][l].astype(F32).reshape(SWA_KV_HEADS, SWA_GROUP, 1)
    lp['sink'] = jnp.broadcast_to(sink[:, :, None, :], (SWA_KV_HEADS, SWA_GROUP, CHUNK, 1)).reshape(
        SWA_KV_HEADS, 1, SWA_GROUP * CHUNK)
    lp['ln_g'] = p['cmlp_ln_g'][l][None, :]
    lp['ln_b'] = p['cmlp_ln_b'][l][None, :]
    ws = p['cmlp_ws'][l]
    pos = jnp.arange(CMLP_CHUNK)
    mask = (pos[None, :] // CHUNK) <= (pos[:, None] // CHUNK)
    lp['cw_full'] = jnp.where(mask[None], ws, 0.0).astype(BF16)
    lp['cw_half'] = ws[:, :CHUNK, :CHUNK].astype(BF16)
    bs = p['cmlp_bs'][l]
    bs_full = jnp.repeat(bs.T, CMLP_GW, axis=1).astype(F32)
    lp['cb_full'] = bs_full
    lp['cb_half'] = bs_full[:CHUNK]
    lp['wmg'] = p['w_merge_gate'][l].astype(BF16)
    lp['bmg'] = p['b_merge_gate'][l][None, :]
    lp['wba'] = p['w_branch_a'][l].astype(BF16)
    lp['wbb'] = p['w_branch_b'][l].astype(BF16)
    lp['wbc'] = p['w_branch_c'][l].astype(BF16)
    lp['wo'] = p['w_out'][l].astype(BF16)
    wr = jnp.concatenate([p['moe_w_re'][l], p['moe_w_rg'][l]], axis=1)
    wr = jnp.pad(wr, ((0, 0), (0, ROUTE_W - wr.shape[1])))
    wrh = wr.astype(BF16)
    lp['wrh'] = wrh
    lp['wrl'] = (wr - wrh.astype(F32)).astype(BF16)
    br = jnp.concatenate([p['moe_b_re'][l], p['moe_b_rg'][l]])
    lp['br'] = jnp.pad(br, (0, ROUTE_W - br.shape[0]))[None, :].astype(F32)
    lp['ewg'] = p['moe_w_gate'][l].astype(BF16)
    lp['ewu'] = p['moe_w_up'][l].astype(BF16)
    lp['ewd'] = p['moe_w_down'][l].astype(BF16)
    lp['n1'] = p['norm1_g'][l][None, :]
    lp['n2'] = p['norm2_g'][l][None, :]
    return lp


def _pick_tile(n, pref):
    t = min(n, pref)
    while n % t:
        t //= 2
    return t


def _run_group(x3, mod, lps, bias, final_g, prompt, state_gla, cache_k, cache_v):
    nseq, seqlen, d = x3.shape
    n = nseq * seqlen
    x = x3.reshape(n, d)
    depth = len(lps)
    tm = _pick_tile(n, TM_PROJ if prompt else TM_PROJ // 2)
    tm_moe = _pick_tile(n, TM_MOE)
    kvw = SWA_KV_HEADS * SWA_HD
    gla_out, k_out, v_out, vc_out = [], [], [], []
    for l, lp in enumerate(lps):
        m6 = mod[l].reshape(nseq, 6, d)
        if prompt:
            mods = [m6[:, j][:, None, :] for j in range(6)]
            pst = seqlen // tm
            pst_moe = seqlen // tm_moe
        else:
            mods = [jnp.broadcast_to(m6[:, j][:, None, :], (nseq, seqlen, d)).reshape(n, d) for j in range(6)]
            pst = None
            pst_moe = None
        sh1, sc1, gt1, sh2, sc2, gt2 = mods
        pa, qb, kv, zc, ga = _inproj_call(x, sh1, sc1, lp['n1'], lp['wa'], lp['wq'], lp['wkv'],
                                          lp['wz'], lp['wg'], tm, pst)
        tb = _pick_tile(seqlen, TB_GLA)
        s0 = None if prompt else state_gla[l]
        oa, s_new = _gla_call(pa, ga, lp['gla_wg'], lp['gla_bg'], lp['gla_ng'], s0, nseq, seqlen, tb)
        if prompt:
            tq = _pick_tile(seqlen, TQ_SWA)
            ob = _swa_call(qb, (kv, 0), (kv, 1), (kv, 0), (kv, 1), bias, lp['sink'], nseq, seqlen, tq, True)
            kv3 = kv.reshape(nseq, seqlen, 2, SWA_KV_HEADS, SWA_HD)
            k_state = kv3[:, -WINDOW:, 0]
            v_state = kv3[:, -WINDOW:, 1]
        else:
            lc = cache_k.shape[2]
            ck = cache_k[l].reshape(nseq, lc, kvw)
            cv = cache_v[l].reshape(nseq, lc, kvw)
            ob = _swa_call(qb, (kv, 0), (kv, 1), (ck, 0), (cv, 0), bias, lp['sink'], nseq, seqlen, seqlen, False)
            kv3 = kv.reshape(nseq, seqlen, 2, SWA_KV_HEADS, SWA_HD)
            k_state = jnp.concatenate([cache_k[l], kv3[:, :, 0]], axis=1)[:, -lc:]
            v_state = jnp.concatenate([cache_v[l], kv3[:, :, 1]], axis=1)[:, -lc:]
        if prompt:
            oc, _ = _cmlp_call(zc, lp['ln_g'], lp['ln_b'], lp['cw_full'], lp['cb_full'],
                               _pick_tile(n, TM_CMLP), CMLP_CHUNK, False)
        else:
            oc, vc = _cmlp_call(zc, lp['ln_g'], lp['ln_b'], lp['cw_half'], lp['cb_half'],
                                _pick_tile(n, TM_CMLP), seqlen, True)
            vc_out.append(vc.reshape(nseq, seqlen, CMLP_WIDTH))
        x1, h2, lg = _merge_call(x, [sh1, sc1, gt1, sh2, sc2], lp['n1'], lp['n2'], oa, ob, oc,
                                 lp['wmg'], lp['bmg'], lp['wba'], lp['wbb'], lp['wbc'], lp['wo'],
                                 lp['wrh'], lp['wrl'], lp['br'], tm, pst)
        pos, w2, meta = _route_call(lg, tm_moe)
        meta_flat = jnp.concatenate([meta[:, 0, :N_EXPERTS], meta[:, 1, :N_EXPERTS]], axis=1).reshape(-1)
        x = _moe_call(pos.reshape(-1), meta_flat, h2, x1, w2, gt2, final_g, lp['ewg'], lp['ewu'], lp['ewd'],
                      tm_moe, C_MOE, EPS_MOE, pst_moe, l == depth - 1)
        gla_out.append(s_new)
        k_out.append(k_state)
        v_out.append(v_state)
    y = x.reshape(nseq, seqlen, d)
    vc_stack = jnp.stack(vc_out) if vc_out else None
    return y, jnp.stack(gla_out), jnp.stack(k_out), jnp.stack(v_out), vc_stack


def kernel(x_prompt, x_sample, state_gla, cache_swa_k, cache_swa_v, c_prompt, c_sample, rel_table, final_norm_g, norm1_g, norm2_g, w_mod, b_mod, w_in, gla_w_gate, gla_b_gate, gla_norm_g, swa_sink, cmlp_ln_g, cmlp_ln_b, cmlp_ws, cmlp_bs, w_branch_a, w_branch_b, w_branch_c, w_merge_gate, b_merge_gate, w_out, moe_w_rg, moe_b_rg, moe_w_re, moe_b_re, moe_w_gate, moe_w_up, moe_w_down):
    p = dict(norm1_g=norm1_g, norm2_g=norm2_g, w_in=w_in, gla_w_gate=gla_w_gate, gla_b_gate=gla_b_gate,
             gla_norm_g=gla_norm_g, swa_sink=swa_sink, cmlp_ln_g=cmlp_ln_g, cmlp_ln_b=cmlp_ln_b,
             cmlp_ws=cmlp_ws, cmlp_bs=cmlp_bs, w_branch_a=w_branch_a, w_branch_b=w_branch_b,
             w_branch_c=w_branch_c, w_merge_gate=w_merge_gate, b_merge_gate=b_merge_gate, w_out=w_out,
             moe_w_rg=moe_w_rg, moe_b_rg=moe_b_rg, moe_w_re=moe_w_re, moe_b_re=moe_b_re,
             moe_w_gate=moe_w_gate, moe_w_up=moe_w_up, moe_w_down=moe_w_down)
    depth = w_in.shape[0]
    nb = c_prompt.shape[0]
    lps = [_layer_params(l, p) for l in range(depth)]
    bias = _swa_bias(rel_table)
    mod = _mod_call(jnp.concatenate([c_prompt, c_sample], axis=0), w_mod, b_mod)
    fg = final_norm_g[None, :]
    y_p, gla_p, k_p, v_p, _ = _run_group(x_prompt, mod[:, :nb], lps, bias, fg, True, None, None, None)
    y_s, gla_s, k_s, v_s, vc_s = _run_group(x_sample, mod[:, nb:], lps, bias, fg, False,
                                            state_gla, cache_swa_k, cache_swa_v)
    return (y_p, y_s, gla_p, k_p, v_p, gla_s, k_s, v_s, vc_s)
```

```python
import functools

import jax
import jax.numpy as jnp
from jax import lax
from jax.experimental import pallas as pl
from jax.experimental.pallas import tpu as pltpu

F32 = jnp.float32
BF16 = jnp.bfloat16
U32 = jnp.uint32
HIGHEST = lax.Precision.HIGHEST

D_MODEL = 1024
CHUNK = 64
EPS = 1e-6
GLA_HEADS = 4
GLA_DK = 128
GLA_DV = 256
GLA_RANK = 16
GLA_TAU = 16.0
SWA_HEADS = 16
SWA_KV_HEADS = 4
SWA_GROUP = 4
SWA_HD = 64
WINDOW = 128
REL_BUCKETS = 32
REL_MAX_DIST = 128
CMLP_CHUNK = 128
CMLP_GROUPS = 4
CMLP_WIDTH = 1024
CMLP_GW = 256
MOE_GROUPS = 4
MOE_PER_GROUP = 8
N_EXPERTS = 32
MOE_HIDDEN = 256

LANES = 128
SUBLANES = 8
VMEM_LIMIT = 56 * 1024 * 1024

TM_PROJ = 512
TB_GLA = 512
TQ_SWA = 256
TM_CMLP = 256
TM_MOE = 2048


def _moe_block_rows(tm):
    return max(16, -(-(tm * 5 // 64) // 16) * 16)
EPS_MOE = 4
ROUTE_W = 128


def _cparams(sem):
    return pltpu.CompilerParams(dimension_semantics=sem, vmem_limit_bytes=VMEM_LIMIT)


def _dot(a, b):
    return jnp.dot(a, b, preferred_element_type=F32)


def _dot_nt(a, b):
    return lax.dot_general(a, b, (((1,), (1,)), ((), ())), preferred_element_type=F32)


def _dot_tn(a, b):
    return lax.dot_general(a, b, (((0,), (0,)), ((), ())), preferred_element_type=F32)


def _sigmoid(x):
    return 1.0 / (1.0 + jnp.exp(-x))


def _modulate(x, g, shift, scale):
    ms = jnp.mean(x * x, axis=-1, keepdims=True)
    return x * lax.rsqrt(ms + EPS) * g * (1.0 + scale) + shift


def _mod_kernel(c_ref, w_ref, b_ref, o_ref):
    c = c_ref[...]
    s = c * _sigmoid(c)
    o_ref[0] = jnp.dot(s, w_ref[0], precision=HIGHEST, preferred_element_type=F32) + b_ref[0]


def _mod_call(c_all, w_mod, b_mod):
    depth, d, n = w_mod.shape
    r = c_all.shape[0]
    bn = 1024
    return pl.pallas_call(
        _mod_kernel,
        grid=(depth, n // bn),
        in_specs=[
            pl.BlockSpec((r, d), lambda l, j: (0, 0)),
            pl.BlockSpec((1, d, bn), lambda l, j: (l, 0, j)),
            pl.BlockSpec((1, 1, bn), lambda l, j: (l, 0, j)),
        ],
        out_specs=pl.BlockSpec((1, r, bn), lambda l, j: (l, 0, j)),
        out_shape=jax.ShapeDtypeStruct((depth, r, n), F32),
        compiler_params=_cparams(("arbitrary", "arbitrary")),
        name="mod",
    )(c_all, w_mod, b_mod.reshape(depth, 1, n))


def _inproj_kernel(x_ref, sh_ref, sc_ref, g_ref, wa_ref, wq_ref, wkv_ref, wz_ref, wg_ref,
                   pa_ref, qb_ref, kv_ref, zc_ref, ga_ref):
    h = _modulate(x_ref[...], g_ref[...], sh_ref[...], sc_ref[...]).astype(BF16)
    col = 512
    for w_ref, o_ref in ((wa_ref, pa_ref), (wq_ref, qb_ref), (wkv_ref, kv_ref), (wz_ref, zc_ref)):
        n = w_ref.shape[1]
        for j in range(0, n, col):
            o_ref[:, j:j + col] = _dot(h, w_ref[:, j:j + col]).astype(o_ref.dtype)
    ga_ref[...] = _dot(h, wg_ref[...])


def _row_spec(tm, width, per_seq_tiles):
    if per_seq_tiles is None:
        return pl.BlockSpec((tm, width), lambda i: (i, 0))
    return pl.BlockSpec((None, 1, width), lambda i: (i // per_seq_tiles, 0, 0))


def _const_spec(shape):
    nd = len(shape)
    return pl.BlockSpec(shape, lambda i: (0,) * nd, pipeline_mode=pl.Buffered(1))


def _inproj_call(x, sh, sc, g, wa, wq, wkv, wz, wg, tm, per_seq_tiles):
    n, d = x.shape
    outs = (
        jax.ShapeDtypeStruct((n, wa.shape[1]), BF16),
        jax.ShapeDtypeStruct((n, wq.shape[1]), BF16),
        jax.ShapeDtypeStruct((n, wkv.shape[1]), F32),
        jax.ShapeDtypeStruct((n, wz.shape[1]), BF16),
        jax.ShapeDtypeStruct((n, LANES), F32),
    )
    return pl.pallas_call(
        _inproj_kernel,
        grid=(n // tm,),
        in_specs=[
            pl.BlockSpec((tm, d), lambda i: (i, 0)),
            _row_spec(tm, d, per_seq_tiles),
            _row_spec(tm, d, per_seq_tiles),
            _const_spec((1, d)),
            _const_spec(wa.shape), _const_spec(wq.shape), _const_spec(wkv.shape),
            _const_spec(wz.shape), _const_spec(wg.shape),
        ],
        out_specs=[pl.BlockSpec((tm, o.shape[1]), lambda i: (i, 0)) for o in outs],
        out_shape=outs,
        compiler_params=_cparams(("arbitrary",)),
        name="inproj",
    )(x, sh, sc, g, wa, wq, wkv, wz, wg)


def _gla_kernel(*refs, nblk, has_state):
    if has_state:
        (q_ref, k_ref, v_ref, r_ref, ga_ref, wg_ref, bg_ref, ng_ref, s0_ref,
         o_ref, sfin_ref, st_ref) = refs
    else:
        (q_ref, k_ref, v_ref, r_ref, ga_ref, wg_ref, bg_ref, ng_ref,
         o_ref, sfin_ref, st_ref) = refs
        s0_ref = None
    t = pl.program_id(2)

    @pl.when(t == 0)
    def _():
        if has_state:
            st_ref[...] = s0_ref[0, 0].T
        else:
            st_ref[...] = jnp.zeros_like(st_ref)

    row = lax.broadcasted_iota(jnp.int32, (CHUNK, CHUNK), 0)
    colm = lax.broadcasted_iota(jnp.int32, (CHUNK, CHUNK), 1)
    causal = row >= colm
    tril = causal.astype(F32)
    gl = _dot(ga_ref[...].astype(BF16), wg_ref[...].astype(BF16)) + bg_ref[...]
    la = (jnp.minimum(gl, 0.0) - jnp.log(1.0 + jnp.exp(-jnp.abs(gl)))) / GLA_TAU
    sls = [slice(c * CHUNK, (c + 1) * CHUNK) for c in range(nblk)]
    cum = [jnp.dot(tril, la[sl], precision=HIGHEST, preferred_element_type=F32) for sl in sls]
    qts, kts, kds, decs = [], [], [], []
    for c, sl in enumerate(sls):
        b = cum[c]
        bl = b[CHUNK - 1:CHUNK, :]
        q = q_ref[sl, :].astype(F32) * (GLA_DK ** -0.5)
        k = k_ref[sl, :].astype(F32)
        qts.append((q * jnp.exp(b)).astype(BF16))
        kts.append((k * jnp.exp(-b)).astype(BF16))
        kds.append((k * jnp.exp(bl - b)).astype(BF16))
        decs.append(jnp.exp(bl))
    amats = [jnp.where(causal, _dot_nt(qts[c], kts[c]), 0.0).astype(BF16) for c in range(nblk)]
    kvs = [_dot_tn(v_ref[sls[c], :], kds[c]) for c in range(nblk)]
    st = st_ref[...]
    states = []
    for c in range(nblk):
        states.append(st.astype(BF16))
        st = st * decs[c] + kvs[c]
    st_ref[...] = st
    for c, sl in enumerate(sls):
        o = _dot(amats[c], v_ref[sl, :]) + _dot_nt(qts[c], states[c])
        on = o * lax.rsqrt(jnp.mean(o * o, axis=-1, keepdims=True) + EPS) * ng_ref[...]
        r = r_ref[sl, :].astype(F32)
        o_ref[sl, :] = (on * (r * _sigmoid(r))).astype(o_ref.dtype)

    @pl.when(t == pl.num_programs(2) - 1)
    def _():
        sfin_ref[0, 0] = st_ref[...].T


def _gla_call(pa, ga, wg, bg, ng, s0, nseq, seqlen, tb):
    n = pa.shape[0]
    nt = seqlen // tb
    kq = GLA_HEADS
    vq = (2 * GLA_HEADS * GLA_DK) // GLA_DV
    rq = vq + GLA_HEADS
    in_specs = [
        pl.BlockSpec((tb, GLA_DK), lambda b, h, t: (b * nt + t, h)),
        pl.BlockSpec((tb, GLA_DK), lambda b, h, t: (b * nt + t, kq + h)),
        pl.BlockSpec((tb, GLA_DV), lambda b, h, t: (b * nt + t, vq + h)),
        pl.BlockSpec((tb, GLA_DV), lambda b, h, t: (b * nt + t, rq + h)),
        pl.BlockSpec((tb, LANES), lambda b, h, t: (b * nt + t, 0)),
        pl.BlockSpec((LANES, GLA_DK), lambda b, h, t: (0, h)),
        pl.BlockSpec((1, GLA_DK), lambda b, h, t: (0, h)),
        pl.BlockSpec((1, GLA_DV), lambda b, h, t: (0, 0)),
    ]
    args = [pa, pa, pa, pa, ga, wg, bg, ng]
    if s0 is not None:
        in_specs.append(pl.BlockSpec((1, 1, GLA_DK, GLA_DV), lambda b, h, t: (b, h, 0, 0)))
        args.append(s0)
    return pl.pallas_call(
        functools.partial(_gla_kernel, nblk=tb // CHUNK, has_state=s0 is not None),
        grid=(nseq, GLA_HEADS, nt),
        in_specs=in_specs,
        out_specs=[
            pl.BlockSpec((tb, GLA_DV), lambda b, h, t: (b * nt + t, h)),
            pl.BlockSpec((1, 1, GLA_DK, GLA_DV), lambda b, h, t: (b, h, 0, 0)),
        ],
        out_shape=(
            jax.ShapeDtypeStruct((n, GLA_HEADS * GLA_DV), BF16),
            jax.ShapeDtypeStruct((nseq, GLA_HEADS, GLA_DK, GLA_DV), F32),
        ),
        scratch_shapes=[pltpu.VMEM((GLA_DV, GLA_DK), F32)],
        compiler_params=_cparams(("arbitrary", "arbitrary", "arbitrary")),
        name="gla",
    )(*args)


def _swa_kernel(q_ref, kc_ref, vc_ref, kp_ref, vp_ref, bias_ref, sink_ref, o_ref, *, tq, mask_start):
    blk = pl.program_id(1)
    kall = jnp.concatenate([kp_ref[...], kc_ref[...]], axis=0).astype(BF16)
    vall = jnp.concatenate([vp_ref[...], vc_ref[...]], axis=0).astype(BF16)
    lwin = WINDOW + CHUNK
    cols = SWA_GROUP * CHUNK
    for c in range(tq // CHUNK):
        need_mask = mask_start and c * CHUNK < WINDOW
        if need_mask:
            first_pos = blk * tq + c * CHUNK - WINDOW
            valid = lax.broadcasted_iota(jnp.int32, (lwin, cols), 0) + first_pos >= 0
        scores = []
        for x in range(SWA_KV_HEADS):
            kw = kall[c * CHUNK:c * CHUNK + lwin, x * SWA_HD:(x + 1) * SWA_HD]
            base = x * SWA_GROUP * SWA_HD
            qs = jnp.concatenate(
                [q_ref[c * CHUNK:(c + 1) * CHUNK, base + g * SWA_HD:base + (g + 1) * SWA_HD]
                 for g in range(SWA_GROUP)], axis=0)
            scores.append(_dot_nt(kw, qs))
        probs = []
        for x in range(SWA_KV_HEADS):
            s = scores[x] + bias_ref[x]
            if need_mask:
                s = jnp.where(valid, s, -1e30)
            sink = sink_ref[x]
            m = jnp.maximum(jnp.max(s, axis=0, keepdims=True), sink)
            p = jnp.exp(s - m)
            den = jnp.sum(p, axis=0, keepdims=True) + jnp.exp(sink - m)
            probs.append((p.astype(BF16), 1.0 / den))
        for x in range(SWA_KV_HEADS):
            vw = vall[c * CHUNK:c * CHUNK + lwin, x * SWA_HD:(x + 1) * SWA_HD]
            base = x * SWA_GROUP * SWA_HD
            vpad = jnp.concatenate([vw, jnp.zeros_like(vw)], axis=1)
            p, rden = probs[x]
            o = (_dot_tn(vpad, p) * rden).T
            o_ref[c * CHUNK:(c + 1) * CHUNK, base:base + SWA_GROUP * SWA_HD] = jnp.concatenate(
                [o[g * CHUNK:(g + 1) * CHUNK, :SWA_HD] for g in range(SWA_GROUP)], axis=1).astype(o_ref.dtype)


def _swa_call(qb, k_cur, v_cur, k_prev, v_prev, bias, sink, nseq, seqlen, tq, prompt):
    n = qb.shape[0]
    nt = seqlen // tq
    kvw = SWA_KV_HEADS * SWA_HD
    per_prev = tq // WINDOW if prompt else None

    def cur_spec(cb):
        return pl.BlockSpec((tq, kvw), lambda b, t: (b * nt + t, cb))

    def prev_spec(cb):
        if prompt:
            nprev = seqlen // WINDOW
            return pl.BlockSpec((WINDOW, kvw),
                                lambda b, t: (b * nprev + jnp.maximum(t * per_prev - 1, 0), cb))
        return pl.BlockSpec((None, WINDOW, kvw), lambda b, t: (b, 0, cb))

    return pl.pallas_call(
        functools.partial(_swa_kernel, tq=tq, mask_start=prompt),
        grid=(nseq, nt),
        in_specs=[
            pl.BlockSpec((tq, SWA_HEADS * SWA_HD), lambda b, t: (b * nt + t, 0)),
            cur_spec(k_cur[1]), cur_spec(v_cur[1]), prev_spec(k_prev[1]), prev_spec(v_prev[1]),
            pl.BlockSpec(bias.shape, lambda b, t: (0, 0, 0)),
            pl.BlockSpec(sink.shape, lambda b, t: (0, 0, 0)),
        ],
        out_specs=pl.BlockSpec((tq, SWA_HEADS * SWA_HD), lambda b, t: (b * nt + t, 0)),
        out_shape=jax.ShapeDtypeStruct((n, SWA_HEADS * SWA_HD), BF16),
        compiler_params=_cparams(("arbitrary", "arbitrary")),
        name="swa",
    )(qb, k_cur[0], v_cur[0], k_prev[0], v_prev[0], bias, sink)


def _cmlp_kernel(*refs, n, emit_v):
    if emit_v:
        z_ref, lg_ref, lb_ref, w_ref, bs_ref, o_ref, v_ref = refs
    else:
        z_ref, lg_ref, lb_ref, w_ref, bs_ref, o_ref = refs
    tm = z_ref.shape[0]
    z = z_ref[...].astype(F32)
    z = 0.5 * z * (1.0 + jnp.tanh(0.7978845608028654 * (z + 0.044715 * (z * z * z))))
    u = z[:, :CMLP_WIDTH]
    v = z[:, CMLP_WIDTH:]
    xc = v - jnp.mean(v, axis=-1, keepdims=True)
    var = jnp.mean(xc * xc, axis=-1, keepdims=True)
    vn = xc * lax.rsqrt(var + EPS) * lg_ref[...] + lb_ref[...]
    if emit_v:
        v_ref[...] = vn
    vb = vn.astype(BF16)
    for ch in range(tm // n):
        rs = slice(ch * n, (ch + 1) * n)
        for g in range(CMLP_GROUPS):
            cs = slice(g * CMLP_GW, (g + 1) * CMLP_GW)
            mix = _dot(w_ref[g], vb[rs, cs]) + bs_ref[:, cs]
            o_ref[rs, cs] = (u[rs, cs] * mix).astype(o_ref.dtype)


def _cmlp_call(zc, lg, lb, w, bs, tm, n, emit_v):
    ntok = zc.shape[0]
    outs = [jax.ShapeDtypeStruct((ntok, CMLP_WIDTH), BF16)]
    if emit_v:
        outs.append(jax.ShapeDtypeStruct((ntok, CMLP_WIDTH), F32))
    res = pl.pallas_call(
        functools.partial(_cmlp_kernel, n=n, emit_v=emit_v),
        grid=(ntok // tm,),
        in_specs=[
            pl.BlockSpec((tm, 2 * CMLP_WIDTH), lambda i: (i, 0)),
            _const_spec((1, CMLP_WIDTH)), _const_spec((1, CMLP_WIDTH)),
            _const_spec(w.shape), _const_spec(bs.shape),
        ],
        out_specs=[pl.BlockSpec((tm, CMLP_WIDTH), lambda i: (i, 0)) for _ in outs],
        out_shape=outs,
        compiler_params=_cparams(("arbitrary",)),
        name="cmlp",
    )(zc, lg, lb, w, bs)
    return res if emit_v else (res[0], None)


def _merge_kernel(x_ref, sh1_ref, sc1_ref, gt1_ref, sh2_ref, sc2_ref, g1_ref, g2_ref,
                  oa_ref, ob_ref, oc_ref, wmg_ref, bmg_ref, wa_ref, wb_ref, wc_ref, wo_ref,
                  wrh_ref, wrl_ref, br_ref, xo_ref, h2_ref, lg_ref):
    x = x_ref[...]
    h = _modulate(x, g1_ref[...], sh1_ref[...], sc1_ref[...]).astype(BF16)
    d = x.shape[1]
    m = None
    for i, (o_ref, w_ref) in enumerate(((oa_ref, wa_ref), (ob_ref, wb_ref), (oc_ref, wc_ref))):
        gate = _sigmoid(_dot(h, wmg_ref[:, i * d:(i + 1) * d]) + bmg_ref[:, i * d:(i + 1) * d])
        term = gate * _dot(o_ref[...], w_ref[...])
        m = term if m is None else m + term
    xn = x + gt1_ref[...] * _dot(m.astype(BF16), wo_ref[...])
    xo_ref[...] = xn
    h2 = _modulate(xn, g2_ref[...], sh2_ref[...], sc2_ref[...])
    hi = h2.astype(BF16)
    hf = hi.astype(F32)
    lo = (h2 - hf).astype(BF16)
    h2_ref[...] = ((lax.bitcast_convert_type(hf[:, :d // 2], U32) >> 16)
                   | (lax.bitcast_convert_type(hf[:, d // 2:], U32) & jnp.uint32(0xFFFF0000)))
    lg_ref[...] = (_dot(hi, wrh_ref[...]) + _dot(lo, wrh_ref[...]) + _dot(hi, wrl_ref[...])) + br_ref[...]


def _merge_call(x, mods, g1, g2, oa, ob, oc, wmg, bmg, wa, wb, wc, wo, wrh, wrl, br, tm, per_seq_tiles):
    n, d = x.shape
    tok = lambda w: pl.BlockSpec((tm, w), lambda i: (i, 0))
    in_specs = [tok(d)] + [_row_spec(tm, d, per_seq_tiles) for _ in mods]
    in_specs += [_const_spec((1, d)), _const_spec((1, d)), tok(d), tok(d), tok(d)]
    in_specs += [_const_spec(a.shape) for a in (wmg, bmg, wa, wb, wc, wo, wrh, wrl, br)]
    return pl.pallas_call(
        _merge_kernel,
        grid=(n // tm,),
        in_specs=in_specs,
        out_specs=[tok(d), tok(d // 2), tok(ROUTE_W)],
        out_shape=(
            jax.ShapeDtypeStruct((n, d), F32),
            jax.ShapeDtypeStruct((n, d // 2), U32),
            jax.ShapeDtypeStruct((n, ROUTE_W), F32),
        ),
        compiler_params=_cparams(("arbitrary",)),
        name="merge",
    )(x, *mods, g1, g2, oa, ob, oc, wmg, bmg, wa, wb, wc, wo, wrh, wrl, br)


def _route_kernel(lg_ref, pos_ref, w_ref, meta_ref):
    lg = lg_ref[...]
    tm = lg.shape[0]
    lane = lax.broadcasted_iota(jnp.int32, lg.shape, 1)
    big = jnp.int32(1 << 20)
    ninf = -jnp.inf
    is_g = (lane >= N_EXPERTS) & (lane < N_EXPERTS + MOE_GROUPS)
    gl = jnp.where(is_g, lg, ninf)
    gmax = jnp.max(gl, axis=-1, keepdims=True)
    g_idx = jnp.min(jnp.where(gl == gmax, lane - N_EXPERTS, big), axis=-1, keepdims=True)
    g_val = 1.0 / jnp.sum(jnp.exp(gl - gmax), axis=-1, keepdims=True)
    in_grp = (lane < N_EXPERTS) & ((lane // MOE_PER_GROUP) == g_idx)
    el = jnp.where(in_grp, lg, ninf)
    v0 = jnp.max(el, axis=-1, keepdims=True)
    i0 = jnp.min(jnp.where(el == v0, lane, big), axis=-1, keepdims=True)
    el1 = jnp.where(lane == i0, ninf, el)
    v1 = jnp.max(el1, axis=-1, keepdims=True)
    i1 = jnp.min(jnp.where(el1 == v1, lane, big), axis=-1, keepdims=True)
    e10 = jnp.exp(v1 - v0)
    w0 = g_val / (1.0 + e10)
    w1 = g_val * e10 / (1.0 + e10)

    hit0 = lane == i0
    hit1 = lane == i1
    assign = (hit0 | hit1)
    ab = assign.astype(BF16)
    sb = min(tm, 256)
    r_i = lax.broadcasted_iota(jnp.int32, (sb, sb), 0)
    c_i = lax.broadcasted_iota(jnp.int32, (sb, sb), 1)
    before = (c_i < r_i).astype(BF16)
    ranks = []
    cnt = jnp.zeros((1, lg.shape[1]), F32)
    for q in range(tm // sb):
        a_q = ab[q * sb:(q + 1) * sb, :]
        ranks.append(_dot(before, a_q) + cnt)
        cnt = cnt + jnp.sum(a_q.astype(F32), axis=0, keepdims=True)
    rank = jnp.concatenate(ranks, axis=0)
    cnt8 = jnp.broadcast_to(cnt, (SUBLANES, lg.shape[1]))
    inc = cnt8
    lane8 = lax.broadcasted_iota(jnp.int32, inc.shape, 1)
    sh = 1
    while sh < N_EXPERTS:
        inc = inc + jnp.where(lane8 >= sh, pltpu.roll(inc, sh, 1), 0.0)
        sh *= 2
    off8 = inc - cnt8
    slot = rank + off8[0:1, :]
    p0 = jnp.sum(jnp.where(hit0, slot, 0.0), axis=-1, keepdims=True)
    p1 = jnp.sum(jnp.where(hit1, slot, 0.0), axis=-1, keepdims=True)
    two = lax.broadcasted_iota(jnp.int32, (tm, 2), 1)
    pos_ref[...] = jnp.where(two == 0, p0, p1).astype(jnp.int32)
    w_ref[...] = jnp.where(two == 0, w0, w1)
    row8 = lax.broadcasted_iota(jnp.int32, inc.shape, 0)
    meta_ref[0] = jnp.where(row8 == 0, off8, cnt8).astype(jnp.int32)


def _route_call(lg, tm):
    n = lg.shape[0]
    nt = n // tm
    return pl.pallas_call(
        _route_kernel,
        grid=(nt,),
        in_specs=[pl.BlockSpec((tm, ROUTE_W), lambda i: (i, 0))],
        out_specs=[
            pl.BlockSpec((tm, 2), lambda i: (i, 0)),
            pl.BlockSpec((tm, 2), lambda i: (i, 0)),
            pl.BlockSpec((1, SUBLANES, ROUTE_W), lambda i: (i, 0, 0)),
        ],
        out_shape=(
            jax.ShapeDtypeStruct((n, 2), jnp.int32),
            jax.ShapeDtypeStruct((n, 2), F32),
            jax.ShapeDtypeStruct((nt, SUBLANES, ROUTE_W), jnp.int32),
        ),
        compiler_params=_cparams(("arbitrary",)),
        name="route",
    )(lg)


def _moe_kernel(pos_ref, meta_ref, h_ref, x_ref, w2_ref, gt_ref, fg_ref, wg_ref, wu_ref, wd_ref,
                o_ref, xt_ref, slot_ref, *, tm, c, eps, final):
    i = pl.program_id(0)
    e = pl.program_id(1)
    nq = D_MODEL // 2 // LANES
    nslot = 2 * tm
    pbase = i * nslot

    def read_rows(ref, r0, nrows):
        return [ref[pl.ds(r0 + s, nrows, stride=nq), :] for s in range(nq)]

    def words_to_f32(ws):
        los = [lax.bitcast_convert_type(w << 16, F32) for w in ws]
        his = [lax.bitcast_convert_type(w & jnp.uint32(0xFFFF0000), F32) for w in ws]
        return jnp.concatenate(los + his, axis=1)

    @pl.when(e == 0)
    def _():
        for s in range(nq):
            xt_ref[pl.ds(s, tm, stride=nq), :] = h_ref[:, s * LANES:(s + 1) * LANES]
        slot_ref[pl.ds(nslot * nq, c * nq), :] = jnp.zeros((c * nq, LANES), U32)

        def disp(j, carry):
            for u in range(8):
                t = j * 8 + u
                row = xt_ref[pl.ds(pl.multiple_of(t * nq, nq), nq), :]
                for k in range(2):
                    p = pos_ref[pbase + 2 * t + k]
                    slot_ref[pl.ds(pl.multiple_of(p * nq, nq), nq), :] = row
            return carry
        lax.fori_loop(0, tm // 8, disp, 0)

    half = D_MODEL // 2
    for j in range(eps):
        ex = e * eps + j
        start = meta_ref[i * 2 * N_EXPERTS + ex]
        cnt = meta_ref[i * 2 * N_EXPERTS + N_EXPERTS + ex]

        def block(b, carry, j=j, start=start, cnt=cnt):
            r0 = pl.multiple_of((start + b * c) * nq, nq)
            ws = read_rows(slot_ref, r0, c)
            xb = words_to_f32(ws).astype(BF16)
            gate = _dot(xb, wg_ref[j])
            hid = (gate * _sigmoid(gate)) * _dot(xb, wu_ref[j])
            yb = _dot(hid.astype(BF16), wd_ref[j]).astype(BF16).astype(F32)
            yw = ((lax.bitcast_convert_type(yb[:, :half], U32) >> 16)
                  | (lax.bitcast_convert_type(yb[:, half:], U32) & jnp.uint32(0xFFFF0000)))
            keep = lax.broadcasted_iota(jnp.int32, (c, 1), 0) < (cnt - b * c)
            for s in range(nq):
                slot_ref[pl.ds(r0 + s, c, stride=nq), :] = jnp.where(
                    keep, yw[:, s * LANES:(s + 1) * LANES], ws[s])
            return carry
        lax.fori_loop(0, (cnt + c - 1) // c, block, 0)

    @pl.when(e == pl.num_programs(1) - 1)
    def _():
        rc = min(tm, 256)
        for k in range(2):
            def comb(j, carry):
                for u in range(8):
                    t = j * 8 + u
                    p = pos_ref[pbase + 2 * t + k]
                    xt_ref[pl.ds(pl.multiple_of(t * nq, nq), nq), :] = (
                        slot_ref[pl.ds(pl.multiple_of(p * nq, nq), nq), :])
                return carry
            lax.fori_loop(0, tm // 8, comb, 0)
            for r in range(0, tm, rc):
                rows = slice(r, r + rc)
                term = words_to_f32(read_rows(xt_ref, r * nq, rc)) * w2_ref[rows, k:k + 1]
                if k == 0:
                    o_ref[rows, :] = term
                else:
                    gate = gt_ref[...] if gt_ref.shape[0] == 1 else gt_ref[rows, :]
                    xn = x_ref[rows, :] + gate * (o_ref[rows, :] + term)
                    if final:
                        xn = xn * lax.rsqrt(jnp.mean(xn * xn, axis=-1, keepdims=True) + EPS) * fg_ref[...]
                    o_ref[rows, :] = xn


def _moe_call(pos, meta, h2, x, w2, gt, fg, wg, wu, wd, tm, c, eps, per_seq_tiles, final):
    n, d = x.shape
    nt = n // tm
    if per_seq_tiles is None:
        gt_spec = pl.BlockSpec((tm, d), lambda i, e, *_: (i, 0))
    else:
        gt_spec = pl.BlockSpec((None, 1, d), lambda i, e, *_: (i // per_seq_tiles, 0, 0))
    grid_spec = pltpu.PrefetchScalarGridSpec(
        num_scalar_prefetch=2,
        grid=(nt, N_EXPERTS // eps),
        in_specs=[
            pl.BlockSpec((tm, d // 2), lambda i, e, *_: (i, 0), pipeline_mode=pl.Buffered(1)),
            pl.BlockSpec((tm, d), lambda i, e, *_: (i, 0), pipeline_mode=pl.Buffered(1)),
            pl.BlockSpec((tm, 2), lambda i, e, *_: (i, 0)),
            gt_spec,
            pl.BlockSpec((1, d), lambda i, e, *_: (0, 0)),
            pl.BlockSpec((eps, d, MOE_HIDDEN), lambda i, e, *_: (e, 0, 0)),
            pl.BlockSpec((eps, d, MOE_HIDDEN), lambda i, e, *_: (e, 0, 0)),
            pl.BlockSpec((eps, MOE_HIDDEN, d), lambda i, e, *_: (e, 0, 0)),
        ],
        out_specs=pl.BlockSpec((tm, d), lambda i, e, *_: (i, 0)),
        scratch_shapes=[
            pltpu.VMEM((tm * (d // 2 // LANES), LANES), U32),
            pltpu.VMEM(((2 * tm + c) * (d // 2 // LANES), LANES), U32),
        ],
    )
    return pl.pallas_call(
        functools.partial(_moe_kernel, tm=tm, c=c, eps=eps, final=final),
        grid_spec=grid_spec,
        out_shape=jax.ShapeDtypeStruct((n, d), F32),
        compiler_params=_cparams(("arbitrary", "arbitrary")),
        name="moe",
    )(pos, meta, h2, x, w2, gt, fg, wg, wu, wd)


def _rel_bucket(rel):
    nb = REL_BUCKETS // 2
    max_exact = nb // 2
    ret = jnp.where(rel > 0, nb, 0)
    nabs = jnp.abs(rel)
    nf = jnp.maximum(nabs, 1).astype(F32)
    large = max_exact + (jnp.log(nf / max_exact) / jnp.log(REL_MAX_DIST / max_exact)
                         * (nb - max_exact)).astype(jnp.int32)
    large = jnp.minimum(large, nb - 1)
    return ret + jnp.where(nabs < max_exact, nabs, large)


def _swa_bias(table):
    lwin = WINDOW + CHUNK
    rel = jnp.arange(lwin)[None, :] - WINDOW - jnp.arange(CHUNK)[:, None]
    b = jnp.transpose(table[_rel_bucket(rel)].astype(F32), (2, 0, 1))
    b = b.reshape(SWA_KV_HEADS, SWA_GROUP * CHUNK, lwin)
    return jnp.transpose(b, (0, 2, 1))


def _layer_params(l, p):
    d = D_MODEL
    w_in = p['w_in'][l]
    o = 0
    cols = {}
    for name, width in (('qa', 512), ('ka', 512), ('va', 1024), ('ga', GLA_RANK), ('ra', 1024),
                        ('qb', 1024), ('kb', 256), ('vb', 256), ('zc', 2048)):
        cols[name] = w_in[:, o:o + width]
        o += width
    lp = {}
    lp['wa'] = jnp.concatenate([cols['qa'], cols['ka'], cols['va'], cols['ra']], axis=1).astype(BF16)
    lp['wq'] = (cols['qb'] * (SWA_HD ** -0.5)).astype(BF16)
    lp['wkv'] = jnp.concatenate([cols['kb'], cols['vb']], axis=1).astype(BF16)
    lp['wz'] = cols['zc'].astype(BF16)
    lp['wg'] = jnp.pad(cols['ga'], ((0, 0), (0, LANES - GLA_RANK))).astype(BF16)
    lp['gla_wg'] = jnp.pad(p['gla_w_gate'][l], ((0, LANES - GLA_RANK), (0, 0)))
    lp['gla_bg'] = p['gla_b_gate'][l][None, :]
    lp['gla_ng'] = p['gla_norm_g'][l][None, :]
    sink = p['swa_sink'][l].astype(F32).reshape(SWA_KV_HEADS, SWA_GROUP, 1)
    lp['sink'] = jnp.broadcast_to(sink[:, :, None, :], (SWA_KV_HEADS, SWA_GROUP, CHUNK, 1)).reshape(
        SWA_KV_HEADS, 1, SWA_GROUP * CHUNK)
    lp['ln_g'] = p['cmlp_ln_g'][l][None, :]
    lp['ln_b'] = p['cmlp_ln_b'][l][None, :]
    ws = p['cmlp_ws'][l]
    pos = jnp.arange(CMLP_CHUNK)
    mask = (pos[None, :] // CHUNK) <= (pos[:, None] // CHUNK)
    lp['cw_full'] = jnp.where(mask[None], ws, 0.0).astype(BF16)
    lp['cw_half'] = ws[:, :CHUNK, :CHUNK].astype(BF16)
    bs = p['cmlp_bs'][l]
    bs_full = jnp.repeat(bs.T, CMLP_GW, axis=1).astype(F32)
    lp['cb_full'] = bs_full
    lp['cb_half'] = bs_full[:CHUNK]
    lp['wmg'] = p['w_merge_gate'][l].astype(BF16)
    lp['bmg'] = p['b_merge_gate'][l][None, :]
    lp['wba'] = p['w_branch_a'][l].astype(BF16)
    lp['wbb'] = p['w_branch_b'][l].astype(BF16)
    lp['wbc'] = p['w_branch_c'][l].astype(BF16)
    lp['wo'] = p['w_out'][l].astype(BF16)
    wr = jnp.concatenate([p['moe_w_re'][l], p['moe_w_rg'][l]], axis=1)
    wr = jnp.pad(wr, ((0, 0), (0, ROUTE_W - wr.shape[1])))
    wrh = wr.astype(BF16)
    lp['wrh'] = wrh
    lp['wrl'] = (wr - wrh.astype(F32)).astype(BF16)
    br = jnp.concatenate([p['moe_b_re'][l], p['moe_b_rg'][l]])
    lp['br'] = jnp.pad(br, (0, ROUTE_W - br.shape[0]))[None, :].astype(F32)
    lp['ewg'] = p['moe_w_gate'][l].astype(BF16)
    lp['ewu'] = p['moe_w_up'][l].astype(BF16)
    lp['ewd'] = p['moe_w_down'][l].astype(BF16)
    lp['n1'] = p['norm1_g'][l][None, :]
    lp['n2'] = p['norm2_g'][l][None, :]
    return lp


def _pick_tile(n, pref):
    t = min(n, pref)
    while n % t:
        t //= 2
    return t


def _run_group(x3, mod, lps, bias, final_g, prompt, state_gla, cache_k, cache_v):
    nseq, seqlen, d = x3.shape
    n = nseq * seqlen
    x = x3.reshape(n, d)
    depth = len(lps)
    tm = _pick_tile(seqlen, TM_PROJ) if prompt else _pick_tile(n, TM_PROJ // 2)
    tm_moe = _pick_tile(seqlen if prompt else n, TM_MOE)
    kvw = SWA_KV_HEADS * SWA_HD
    gla_out, k_out, v_out, vc_out = [], [], [], []
    for l, lp in enumerate(lps):
        m6 = mod[l].reshape(nseq, 6, d)
        if prompt:
            mods = [m6[:, j][:, None, :] for j in range(6)]
            pst = seqlen // tm
            pst_moe = seqlen // tm_moe
        else:
            mods = [jnp.broadcast_to(m6[:, j][:, None, :], (nseq, seqlen, d)).reshape(n, d) for j in range(6)]
            pst = None
            pst_moe = None
        sh1, sc1, gt1, sh2, sc2, gt2 = mods
        pa, qb, kv, zc, ga = _inproj_call(x, sh1, sc1, lp['n1'], lp['wa'], lp['wq'], lp['wkv'],
                                          lp['wz'], lp['wg'], tm, pst)
        tb = _pick_tile(seqlen, TB_GLA)
        s0 = None if prompt else state_gla[l]
        oa, s_new = _gla_call(pa, ga, lp['gla_wg'], lp['gla_bg'], lp['gla_ng'], s0, nseq, seqlen, tb)
        if prompt:
            tq = _pick_tile(seqlen, TQ_SWA)
            ob = _swa_call(qb, (kv, 0), (kv, 1), (kv, 0), (kv, 1), bias, lp['sink'], nseq, seqlen, tq, True)
            kv3 = kv.reshape(nseq, seqlen, 2, SWA_KV_HEADS, SWA_HD)
            k_state = kv3[:, -WINDOW:, 0]
            v_state = kv3[:, -WINDOW:, 1]
        else:
            lc = cache_k.shape[2]
            ck = cache_k[l].reshape(nseq, lc, kvw)
            cv = cache_v[l].reshape(nseq, lc, kvw)
            ob = _swa_call(qb, (kv, 0), (kv, 1), (ck, 0), (cv, 0), bias, lp['sink'], nseq, seqlen, seqlen, False)
            kv3 = kv.reshape(nseq, seqlen, 2, SWA_KV_HEADS, SWA_HD)
            k_state = jnp.concatenate([cache_k[l], kv3[:, :, 0]], axis=1)[:, -lc:]
            v_state = jnp.concatenate([cache_v[l], kv3[:, :, 1]], axis=1)[:, -lc:]
        if prompt:
            oc, _ = _cmlp_call(zc, lp['ln_g'], lp['ln_b'], lp['cw_full'], lp['cb_full'],
                               _pick_tile(n, TM_CMLP), CMLP_CHUNK, False)
        else:
            oc, vc = _cmlp_call(zc, lp['ln_g'], lp['ln_b'], lp['cw_half'], lp['cb_half'],
                                _pick_tile(n, TM_CMLP), seqlen, True)
            vc_out.append(vc.reshape(nseq, seqlen, CMLP_WIDTH))
        x1, h2, lg = _merge_call(x, [sh1, sc1, gt1, sh2, sc2], lp['n1'], lp['n2'], oa, ob, oc,
                                 lp['wmg'], lp['bmg'], lp['wba'], lp['wbb'], lp['wbc'], lp['wo'],
                                 lp['wrh'], lp['wrl'], lp['br'], tm, pst)
        pos, w2, meta = _route_call(lg, tm_moe)
        meta_flat = jnp.concatenate([meta[:, 0, :N_EXPERTS], meta[:, 1, :N_EXPERTS]], axis=1).reshape(-1)
        x = _moe_call(pos.reshape(-1), meta_flat, h2, x1, w2, gt2, final_g, lp['ewg'], lp['ewu'], lp['ewd'],
                      tm_moe, _moe_block_rows(tm_moe), EPS_MOE, pst_moe, l == depth - 1)
        gla_out.append(s_new)
        k_out.append(k_state)
        v_out.append(v_state)
    y = x.reshape(nseq, seqlen, d)
    vc_stack = jnp.stack(vc_out) if vc_out else None
    return y, jnp.stack(gla_out), jnp.stack(k_out), jnp.stack(v_out), vc_stack


def kernel(x_prompt, x_sample, state_gla, cache_swa_k, cache_swa_v, c_prompt, c_sample, rel_table, final_norm_g, norm1_g, norm2_g, w_mod, b_mod, w_in, gla_w_gate, gla_b_gate, gla_norm_g, swa_sink, cmlp_ln_g, cmlp_ln_b, cmlp_ws, cmlp_bs, w_branch_a, w_branch_b, w_branch_c, w_merge_gate, b_merge_gate, w_out, moe_w_rg, moe_b_rg, moe_w_re, moe_b_re, moe_w_gate, moe_w_up, moe_w_down):
    p = dict(norm1_g=norm1_g, norm2_g=norm2_g, w_in=w_in, gla_w_gate=gla_w_gate, gla_b_gate=gla_b_gate,
             gla_norm_g=gla_norm_g, swa_sink=swa_sink, cmlp_ln_g=cmlp_ln_g, cmlp_ln_b=cmlp_ln_b,
             cmlp_ws=cmlp_ws, cmlp_bs=cmlp_bs, w_branch_a=w_branch_a, w_branch_b=w_branch_b,
             w_branch_c=w_branch_c, w_merge_gate=w_merge_gate, b_merge_gate=b_merge_gate, w_out=w_out,
             moe_w_rg=moe_w_rg, moe_b_rg=moe_b_rg, moe_w_re=moe_w_re, moe_b_re=moe_b_re,
             moe_w_gate=moe_w_gate, moe_w_up=moe_w_up, moe_w_down=moe_w_down)
    depth = w_in.shape[0]
    nb = c_prompt.shape[0]
    lps = [_layer_params(l, p) for l in range(depth)]
    bias = _swa_bias(rel_table)
    mod = _mod_call(jnp.concatenate([c_prompt, c_sample], axis=0), w_mod, b_mod)
    fg = final_norm_g[None, :]
    y_p, gla_p, k_p, v_p, _ = _run_group(x_prompt, mod[:, :nb], lps, bias, fg, True, None, None, None)
    y_s, gla_s, k_s, v_s, vc_s = _run_group(x_sample, mod[:, nb:], lps, bias, fg, False,
                                            state_gla, cache_swa_k, cache_swa_v)
    return (y_p, y_s, gla_p, k_p, v_p, gla_s, k_s, v_s, vc_s)
```

```python
import functools

import jax
import jax.numpy as jnp
from jax import lax
from jax.experimental import pallas as pl
from jax.experimental.pallas import tpu as pltpu

F32 = jnp.float32
BF16 = jnp.bfloat16
U32 = jnp.uint32
HIGHEST = lax.Precision.HIGHEST

D_MODEL = 1024
CHUNK = 64
EPS = 1e-6
GLA_HEADS = 4
GLA_DK = 128
GLA_DV = 256
GLA_RANK = 16
GLA_TAU = 16.0
SWA_HEADS = 16
SWA_KV_HEADS = 4
SWA_GROUP = 4
SWA_HD = 64
WINDOW = 128
REL_BUCKETS = 32
REL_MAX_DIST = 128
CMLP_CHUNK = 128
CMLP_GROUPS = 4
CMLP_WIDTH = 1024
CMLP_GW = 256
MOE_GROUPS = 4
MOE_PER_GROUP = 8
N_EXPERTS = 32
MOE_HIDDEN = 256

LANES = 128
SUBLANES = 8
VMEM_LIMIT = 56 * 1024 * 1024

TM_PROJ = 512
TB_GLA = 2048
TQ_SWA = 512
TM_MOE = 2048
EPS_MOE = 4


def _moe_block_rows(tm):
    return max(16, -(-(tm * 5 // 64) // 16) * 16)
ROUTE_W = 128


def _cparams(sem):
    return pltpu.CompilerParams(dimension_semantics=sem, vmem_limit_bytes=VMEM_LIMIT)


def _dot(a, b):
    return jnp.dot(a, b, preferred_element_type=F32)


def _dot_nt(a, b):
    return lax.dot_general(a, b, (((1,), (1,)), ((), ())), preferred_element_type=F32)


def _dot_tn(a, b):
    return lax.dot_general(a, b, (((0,), (0,)), ((), ())), preferred_element_type=F32)


def _sigmoid(x):
    return 1.0 / (1.0 + jnp.exp(-x))


def _modulate(x, g, shift, scale):
    ms = jnp.mean(x * x, axis=-1, keepdims=True)
    return x * lax.rsqrt(ms + EPS) * g * (1.0 + scale) + shift


def _mod_kernel(c_ref, w_ref, b_ref, o_ref):
    c = c_ref[...]
    s = c * _sigmoid(c)
    o_ref[0] = jnp.dot(s, w_ref[0], precision=HIGHEST, preferred_element_type=F32) + b_ref[0]


def _mod_call(c_all, w_mod, b_mod):
    depth, d, n = w_mod.shape
    r = c_all.shape[0]
    bn = 1024
    return pl.pallas_call(
        _mod_kernel,
        grid=(depth, n // bn),
        in_specs=[
            pl.BlockSpec((r, d), lambda l, j: (0, 0)),
            pl.BlockSpec((1, d, bn), lambda l, j: (l, 0, j)),
            pl.BlockSpec((1, 1, bn), lambda l, j: (l, 0, j)),
        ],
        out_specs=pl.BlockSpec((1, r, bn), lambda l, j: (l, 0, j)),
        out_shape=jax.ShapeDtypeStruct((depth, r, n), F32),
        compiler_params=_cparams(("arbitrary", "arbitrary")),
        name="mod",
    )(c_all, w_mod, b_mod.reshape(depth, 1, n))


def _inproj_kernel(*refs, n, emit_v):
    if emit_v:
        (x_ref, sh_ref, sc_ref, g_ref, wa_ref, wq_ref, wkv_ref, wz_ref, wg_ref, lg_ref, lb_ref, cw_ref, cb_ref,
         pa_ref, qb_ref, kv_ref, oc_ref, ga_ref, v_ref) = refs
    else:
        (x_ref, sh_ref, sc_ref, g_ref, wa_ref, wq_ref, wkv_ref, wz_ref, wg_ref, lg_ref, lb_ref, cw_ref, cb_ref,
         pa_ref, qb_ref, kv_ref, oc_ref, ga_ref) = refs
    tm = x_ref.shape[0]
    h = _modulate(x_ref[...], g_ref[...], sh_ref[...], sc_ref[...]).astype(BF16)
    col = 512

    def proj(w_ref, o_ref):
        for j in range(0, w_ref.shape[1], col):
            o_ref[:, j:j + col] = _dot(h, w_ref[:, j:j + col]).astype(o_ref.dtype)

    zs = []
    for j in range(0, 2 * CMLP_WIDTH, col):
        z = _dot(h, wz_ref[:, j:j + col])
        zs.append(0.5 * z * (1.0 + jnp.tanh(0.7978845608028654 * (z + 0.044715 * (z * z * z)))))
    u = jnp.concatenate(zs[:len(zs) // 2], axis=1)
    v = jnp.concatenate(zs[len(zs) // 2:], axis=1)
    proj(wa_ref, pa_ref)
    xc = v - jnp.mean(v, axis=-1, keepdims=True)
    var = jnp.mean(xc * xc, axis=-1, keepdims=True)
    vn = xc * lax.rsqrt(var + EPS) * lg_ref[...] + lb_ref[...]
    if emit_v:
        v_ref[...] = vn
    vb = vn.astype(BF16)
    for ch in range(tm // n):
        rs = slice(ch * n, (ch + 1) * n)
        for g in range(CMLP_GROUPS):
            cs = slice(g * CMLP_GW, (g + 1) * CMLP_GW)
            mix = _dot(cw_ref[g], vb[rs, cs]) + cb_ref[:, cs]
            oc_ref[rs, cs] = (u[rs, cs] * mix).astype(oc_ref.dtype)
    proj(wq_ref, qb_ref)
    proj(wkv_ref, kv_ref)
    ga_ref[...] = _dot(h, wg_ref[...])


def _row_spec(tm, width, per_seq_tiles):
    if per_seq_tiles is None:
        return pl.BlockSpec((tm, width), lambda i: (i, 0))
    return pl.BlockSpec((None, 1, width), lambda i: (i // per_seq_tiles, 0, 0))


def _const_spec(shape):
    nd = len(shape)
    return pl.BlockSpec(shape, lambda i: (0,) * nd, pipeline_mode=pl.Buffered(1))


def _inproj_call(x, sh, sc, g, wa, wq, wkv, wz, wg, lg, lb, cw, cb, tm, per_seq_tiles, n, emit_v):
    ntok, d = x.shape
    outs = [
        jax.ShapeDtypeStruct((ntok, wa.shape[1]), BF16),
        jax.ShapeDtypeStruct((ntok, wq.shape[1]), BF16),
        jax.ShapeDtypeStruct((ntok, wkv.shape[1]), F32),
        jax.ShapeDtypeStruct((ntok, CMLP_WIDTH), BF16),
        jax.ShapeDtypeStruct((ntok, LANES), F32),
    ]
    if emit_v:
        outs.append(jax.ShapeDtypeStruct((ntok, CMLP_WIDTH), F32))
    res = pl.pallas_call(
        functools.partial(_inproj_kernel, n=n, emit_v=emit_v),
        grid=(ntok // tm,),
        in_specs=[
            pl.BlockSpec((tm, d), lambda i: (i, 0)),
            _row_spec(tm, d, per_seq_tiles),
            _row_spec(tm, d, per_seq_tiles),
            _const_spec((1, d)),
            _const_spec(wa.shape), _const_spec(wq.shape), _const_spec(wkv.shape),
            _const_spec(wz.shape), _const_spec(wg.shape),
            _const_spec((1, CMLP_WIDTH)), _const_spec((1, CMLP_WIDTH)),
            _const_spec(cw.shape), _const_spec(cb.shape),
        ],
        out_specs=[pl.BlockSpec((tm, o.shape[1]), lambda i: (i, 0)) for o in outs],
        out_shape=outs,
        compiler_params=_cparams(("arbitrary",)),
        name="inproj",
    )(x, sh, sc, g, wa, wq, wkv, wz, wg, lg, lb, cw, cb)
    return res if emit_v else (*res, None)


def _gla_kernel(*refs, nblk, has_state):
    if has_state:
        (q_ref, k_ref, v_ref, r_ref, ga_ref, wg_ref, bg_ref, ng_ref, s0_ref,
         o_ref, sfin_ref, st_ref) = refs
    else:
        (q_ref, k_ref, v_ref, r_ref, ga_ref, wg_ref, bg_ref, ng_ref,
         o_ref, sfin_ref, st_ref) = refs
        s0_ref = None
    t = pl.program_id(2)

    @pl.when(t == 0)
    def _():
        if has_state:
            st_ref[...] = s0_ref[0, 0].T
        else:
            st_ref[...] = jnp.zeros_like(st_ref)

    row = lax.broadcasted_iota(jnp.int32, (CHUNK, CHUNK), 0)
    colm = lax.broadcasted_iota(jnp.int32, (CHUNK, CHUNK), 1)
    causal = row >= colm
    tril = causal.astype(F32)
    gl = _dot(ga_ref[...].astype(BF16), wg_ref[...].astype(BF16)) + bg_ref[...]
    la = (jnp.minimum(gl, 0.0) - jnp.log(1.0 + jnp.exp(-jnp.abs(gl)))) / GLA_TAU
    sls = [slice(c * CHUNK, (c + 1) * CHUNK) for c in range(nblk)]
    cum = [jnp.dot(tril, la[sl], precision=HIGHEST, preferred_element_type=F32) for sl in sls]
    qts, kts, kds, decs = [], [], [], []
    for c, sl in enumerate(sls):
        b = cum[c]
        bl = b[CHUNK - 1:CHUNK, :]
        q = q_ref[sl, :].astype(F32) * (GLA_DK ** -0.5)
        k = k_ref[sl, :].astype(F32)
        qts.append((q * jnp.exp(b)).astype(BF16))
        kts.append((k * jnp.exp(-b)).astype(BF16))
        kds.append((k * jnp.exp(bl - b)).astype(BF16))
        decs.append(jnp.exp(bl))
    amats = [jnp.where(causal, _dot_nt(qts[c], kts[c]), 0.0).astype(BF16) for c in range(nblk)]
    kvs = [_dot_tn(v_ref[sls[c], :], kds[c]) for c in range(nblk)]
    st = st_ref[...]
    states = []
    for c in range(nblk):
        states.append(st.astype(BF16))
        st = st * decs[c] + kvs[c]
    st_ref[...] = st
    for c, sl in enumerate(sls):
        o = _dot(amats[c], v_ref[sl, :]) + _dot_nt(qts[c], states[c])
        on = o * lax.rsqrt(jnp.mean(o * o, axis=-1, keepdims=True) + EPS) * ng_ref[...]
        r = r_ref[sl, :].astype(F32)
        o_ref[sl, :] = (on * (r * _sigmoid(r))).astype(o_ref.dtype)

    @pl.when(t == pl.num_programs(2) - 1)
    def _():
        sfin_ref[0, 0] = st_ref[...].T


def _gla_call(pa, ga, wg, bg, ng, s0, nseq, seqlen, tb):
    n = pa.shape[0]
    nt = seqlen // tb
    kq = GLA_HEADS
    vq = (2 * GLA_HEADS * GLA_DK) // GLA_DV
    rq = vq + GLA_HEADS
    in_specs = [
        pl.BlockSpec((tb, GLA_DK), lambda b, h, t: (b * nt + t, h)),
        pl.BlockSpec((tb, GLA_DK), lambda b, h, t: (b * nt + t, kq + h)),
        pl.BlockSpec((tb, GLA_DV), lambda b, h, t: (b * nt + t, vq + h)),
        pl.BlockSpec((tb, GLA_DV), lambda b, h, t: (b * nt + t, rq + h)),
        pl.BlockSpec((tb, LANES), lambda b, h, t: (b * nt + t, 0)),
        pl.BlockSpec((LANES, GLA_DK), lambda b, h, t: (0, h)),
        pl.BlockSpec((1, GLA_DK), lambda b, h, t: (0, h)),
        pl.BlockSpec((1, GLA_DV), lambda b, h, t: (0, 0)),
    ]
    args = [pa, pa, pa, pa, ga, wg, bg, ng]
    if s0 is not None:
        in_specs.append(pl.BlockSpec((1, 1, GLA_DK, GLA_DV), lambda b, h, t: (b, h, 0, 0)))
        args.append(s0)
    return pl.pallas_call(
        functools.partial(_gla_kernel, nblk=tb // CHUNK, has_state=s0 is not None),
        grid=(nseq, GLA_HEADS, nt),
        in_specs=in_specs,
        out_specs=[
            pl.BlockSpec((tb, GLA_DV), lambda b, h, t: (b * nt + t, h)),
            pl.BlockSpec((1, 1, GLA_DK, GLA_DV), lambda b, h, t: (b, h, 0, 0)),
        ],
        out_shape=(
            jax.ShapeDtypeStruct((n, GLA_HEADS * GLA_DV), BF16),
            jax.ShapeDtypeStruct((nseq, GLA_HEADS, GLA_DK, GLA_DV), F32),
        ),
        scratch_shapes=[pltpu.VMEM((GLA_DV, GLA_DK), F32)],
        compiler_params=_cparams(("arbitrary", "arbitrary", "arbitrary")),
        name="gla",
    )(*args)


def _swa_kernel(q_ref, kc_ref, vc_ref, kp_ref, vp_ref, bias_ref, sink_ref, o_ref, *, tq, mask_start):
    blk = pl.program_id(1)
    kall = jnp.concatenate([kp_ref[...], kc_ref[...]], axis=0).astype(BF16)
    vall = jnp.concatenate([vp_ref[...], vc_ref[...]], axis=0).astype(BF16)
    lwin = WINDOW + CHUNK
    cols = SWA_GROUP * CHUNK
    for c in range(tq // CHUNK):
        need_mask = mask_start and c * CHUNK < WINDOW
        if need_mask:
            first_pos = blk * tq + c * CHUNK - WINDOW
            valid = lax.broadcasted_iota(jnp.int32, (lwin, cols), 0) + first_pos >= 0
        scores = []
        for x in range(SWA_KV_HEADS):
            kw = kall[c * CHUNK:c * CHUNK + lwin, x * SWA_HD:(x + 1) * SWA_HD]
            base = x * SWA_GROUP * SWA_HD
            qs = jnp.concatenate(
                [q_ref[c * CHUNK:(c + 1) * CHUNK, base + g * SWA_HD:base + (g + 1) * SWA_HD]
                 for g in range(SWA_GROUP)], axis=0)
            scores.append(_dot_nt(kw, qs))
        probs = []
        for x in range(SWA_KV_HEADS):
            s = scores[x] + bias_ref[x]
            if need_mask:
                s = jnp.where(valid, s, -1e30)
            sink = sink_ref[x]
            m = jnp.maximum(jnp.max(s, axis=0, keepdims=True), sink)
            p = jnp.exp(s - m)
            den = jnp.sum(p, axis=0, keepdims=True) + jnp.exp(sink - m)
            probs.append((p.astype(BF16), 1.0 / den))
        for x in range(SWA_KV_HEADS):
            vw = vall[c * CHUNK:c * CHUNK + lwin, x * SWA_HD:(x + 1) * SWA_HD]
            base = x * SWA_GROUP * SWA_HD
            vpad = jnp.concatenate([vw, jnp.zeros_like(vw)], axis=1)
            p, rden = probs[x]
            o = (_dot_tn(vpad, p) * rden).T
            o_ref[c * CHUNK:(c + 1) * CHUNK, base:base + SWA_GROUP * SWA_HD] = jnp.concatenate(
                [o[g * CHUNK:(g + 1) * CHUNK, :SWA_HD] for g in range(SWA_GROUP)], axis=1).astype(o_ref.dtype)


def _swa_call(qb, k_cur, v_cur, k_prev, v_prev, bias, sink, nseq, seqlen, tq, prompt):
    n = qb.shape[0]
    nt = seqlen // tq
    kvw = SWA_KV_HEADS * SWA_HD
    per_prev = tq // WINDOW if prompt else None

    def cur_spec(cb):
        return pl.BlockSpec((tq, kvw), lambda b, t: (b * nt + t, cb))

    def prev_spec(cb):
        if prompt:
            nprev = seqlen // WINDOW
            return pl.BlockSpec((WINDOW, kvw),
                                lambda b, t: (b * nprev + jnp.maximum(t * per_prev - 1, 0), cb))
        return pl.BlockSpec((None, WINDOW, kvw), lambda b, t: (b, 0, cb))

    return pl.pallas_call(
        functools.partial(_swa_kernel, tq=tq, mask_start=prompt),
        grid=(nseq, nt),
        in_specs=[
            pl.BlockSpec((tq, SWA_HEADS * SWA_HD), lambda b, t: (b * nt + t, 0)),
            cur_spec(k_cur[1]), cur_spec(v_cur[1]), prev_spec(k_prev[1]), prev_spec(v_prev[1]),
            pl.BlockSpec(bias.shape, lambda b, t: (0, 0, 0)),
            pl.BlockSpec(sink.shape, lambda b, t: (0, 0, 0)),
        ],
        out_specs=pl.BlockSpec((tq, SWA_HEADS * SWA_HD), lambda b, t: (b * nt + t, 0)),
        out_shape=jax.ShapeDtypeStruct((n, SWA_HEADS * SWA_HD), BF16),
        compiler_params=_cparams(("arbitrary", "arbitrary")),
        name="swa",
    )(qb, k_cur[0], v_cur[0], k_prev[0], v_prev[0], bias, sink)


def _merge_kernel(x_ref, sh1_ref, sc1_ref, gt1_ref, sh2_ref, sc2_ref, g1_ref, g2_ref,
                  oa_ref, ob_ref, oc_ref, wmg_ref, bmg_ref, wa_ref, wb_ref, wc_ref, wo_ref,
                  wrh_ref, wrw_ref, br_ref, xo_ref, h2_ref, lg_ref):
    x = x_ref[...]
    h = _modulate(x, g1_ref[...], sh1_ref[...], sc1_ref[...]).astype(BF16)
    d = x.shape[1]
    m = None
    for i, (o_ref, w_ref) in enumerate(((oa_ref, wa_ref), (ob_ref, wb_ref), (oc_ref, wc_ref))):
        gate = _sigmoid(_dot(h, wmg_ref[:, i * d:(i + 1) * d]) + bmg_ref[:, i * d:(i + 1) * d])
        term = gate * _dot(o_ref[...], w_ref[...])
        m = term if m is None else m + term
    xn = x + gt1_ref[...] * _dot(m.astype(BF16), wo_ref[...])
    xo_ref[...] = xn
    h2 = _modulate(xn, g2_ref[...], sh2_ref[...], sc2_ref[...])
    hi = h2.astype(BF16)
    hf = hi.astype(F32)
    lo = (h2 - hf).astype(BF16)
    h2_ref[...] = ((lax.bitcast_convert_type(hf[:, :d // 2], U32) >> 16)
                   | (lax.bitcast_convert_type(hf[:, d // 2:], U32) & jnp.uint32(0xFFFF0000)))
    both = _dot(hi, wrw_ref[...])
    lg_ref[...] = (both[:, :ROUTE_W] + both[:, ROUTE_W:] + _dot(lo, wrh_ref[...])) + br_ref[...]


def _merge_call(x, mods, g1, g2, oa, ob, oc, wmg, bmg, wa, wb, wc, wo, wrh, wrw, br, tm, per_seq_tiles):
    n, d = x.shape
    tok = lambda w: pl.BlockSpec((tm, w), lambda i: (i, 0))
    in_specs = [tok(d)] + [_row_spec(tm, d, per_seq_tiles) for _ in mods]
    in_specs += [_const_spec((1, d)), _const_spec((1, d)), tok(d), tok(d), tok(d)]
    in_specs += [_const_spec(a.shape) for a in (wmg, bmg, wa, wb, wc, wo, wrh, wrw, br)]
    return pl.pallas_call(
        _merge_kernel,
        grid=(n // tm,),
        in_specs=in_specs,
        out_specs=[tok(d), tok(d // 2), tok(ROUTE_W)],
        out_shape=(
            jax.ShapeDtypeStruct((n, d), F32),
            jax.ShapeDtypeStruct((n, d // 2), U32),
            jax.ShapeDtypeStruct((n, ROUTE_W), F32),
        ),
        compiler_params=_cparams(("arbitrary",)),
        name="merge",
    )(x, *mods, g1, g2, oa, ob, oc, wmg, bmg, wa, wb, wc, wo, wrh, wrw, br)


def _route_kernel(lg_ref, pos_ref, w_ref, meta_ref):
    lg = lg_ref[...]
    tm = lg.shape[0]
    lane = lax.broadcasted_iota(jnp.int32, lg.shape, 1)
    big = jnp.int32(1 << 20)
    ninf = -jnp.inf
    is_g = (lane >= N_EXPERTS) & (lane < N_EXPERTS + MOE_GROUPS)
    gl = jnp.where(is_g, lg, ninf)
    gmax = jnp.max(gl, axis=-1, keepdims=True)
    g_idx = jnp.min(jnp.where(gl == gmax, lane - N_EXPERTS, big), axis=-1, keepdims=True)
    g_val = 1.0 / jnp.sum(jnp.exp(gl - gmax), axis=-1, keepdims=True)
    in_grp = (lane < N_EXPERTS) & ((lane // MOE_PER_GROUP) == g_idx)
    el = jnp.where(in_grp, lg, ninf)
    v0 = jnp.max(el, axis=-1, keepdims=True)
    i0 = jnp.min(jnp.where(el == v0, lane, big), axis=-1, keepdims=True)
    el1 = jnp.where(lane == i0, ninf, el)
    v1 = jnp.max(el1, axis=-1, keepdims=True)
    i1 = jnp.min(jnp.where(el1 == v1, lane, big), axis=-1, keepdims=True)
    e10 = jnp.exp(v1 - v0)
    w0 = g_val / (1.0 + e10)
    w1 = g_val * e10 / (1.0 + e10)

    hit0 = lane == i0
    hit1 = lane == i1
    assign = (hit0 | hit1)
    ab = assign.astype(BF16)
    sb = min(tm, 256)
    r_i = lax.broadcasted_iota(jnp.int32, (sb, sb), 0)
    c_i = lax.broadcasted_iota(jnp.int32, (sb, sb), 1)
    before = (c_i < r_i).astype(BF16)
    ranks = []
    cnt = jnp.zeros((1, lg.shape[1]), F32)
    for q in range(tm // sb):
        a_q = ab[q * sb:(q + 1) * sb, :]
        ranks.append(_dot(before, a_q) + cnt)
        cnt = cnt + jnp.sum(a_q.astype(F32), axis=0, keepdims=True)
    rank = jnp.concatenate(ranks, axis=0)
    cnt8 = jnp.broadcast_to(cnt, (SUBLANES, lg.shape[1]))
    inc = cnt8
    lane8 = lax.broadcasted_iota(jnp.int32, inc.shape, 1)
    sh = 1
    while sh < N_EXPERTS:
        inc = inc + jnp.where(lane8 >= sh, pltpu.roll(inc, sh, 1), 0.0)
        sh *= 2
    off8 = inc - cnt8
    slot = rank + off8[0:1, :]
    p0 = jnp.sum(jnp.where(hit0, slot, 0.0), axis=-1, keepdims=True)
    p1 = jnp.sum(jnp.where(hit1, slot, 0.0), axis=-1, keepdims=True)
    pos_ref[...] = jnp.where(lane == 0, p0, jnp.where(lane == 1, p1, 0.0)).astype(jnp.int32)
    w_ref[...] = jnp.where(lane == 0, w0, jnp.where(lane == 1, w1, 0.0))
    row8 = lax.broadcasted_iota(jnp.int32, inc.shape, 0)
    meta_ref[0] = jnp.where(row8 == 0, off8, cnt8).astype(jnp.int32)


def _route_call(lg, tm):
    n = lg.shape[0]
    nt = n // tm
    return pl.pallas_call(
        _route_kernel,
        grid=(nt,),
        in_specs=[pl.BlockSpec((tm, ROUTE_W), lambda i: (i, 0))],
        out_specs=[
            pl.BlockSpec((tm, ROUTE_W), lambda i: (i, 0)),
            pl.BlockSpec((tm, ROUTE_W), lambda i: (i, 0)),
            pl.BlockSpec((1, SUBLANES, ROUTE_W), lambda i: (i, 0, 0)),
        ],
        out_shape=(
            jax.ShapeDtypeStruct((n, ROUTE_W), jnp.int32),
            jax.ShapeDtypeStruct((n, ROUTE_W), F32),
            jax.ShapeDtypeStruct((nt, SUBLANES, ROUTE_W), jnp.int32),
        ),
        compiler_params=_cparams(("arbitrary",)),
        name="route",
    )(lg)


def _moe_kernel(pos_ref, meta_ref, h_ref, x_ref, w2_ref, gt_ref, fg_ref, wg_ref, wu_ref, wd_ref,
                o_ref, xt_ref, slot_ref, *, tm, c, eps, final):
    i = pl.program_id(0)
    e = pl.program_id(1)
    nq = D_MODEL // 2 // LANES
    nslot = 2 * tm
    pbase = i * nslot

    def read_rows(ref, r0, nrows):
        return [ref[pl.ds(r0 + s, nrows, stride=nq), :] for s in range(nq)]

    def words_to_f32(ws):
        los = [lax.bitcast_convert_type(w << 16, F32) for w in ws]
        his = [lax.bitcast_convert_type(w & jnp.uint32(0xFFFF0000), F32) for w in ws]
        return jnp.concatenate(los + his, axis=1)

    @pl.when(e == 0)
    def _():
        for s in range(nq):
            xt_ref[pl.ds(s, tm, stride=nq), :] = h_ref[:, s * LANES:(s + 1) * LANES]
        slot_ref[pl.ds(nslot * nq, c * nq), :] = jnp.zeros((c * nq, LANES), U32)

        def disp(j, carry):
            for u in range(8):
                t = j * 8 + u
                row = xt_ref[pl.ds(pl.multiple_of(t * nq, nq), nq), :]
                for k in range(2):
                    p = pos_ref[pbase + 2 * t + k]
                    slot_ref[pl.ds(pl.multiple_of(p * nq, nq), nq), :] = row
            return carry
        lax.fori_loop(0, tm // 8, disp, 0)

    half = D_MODEL // 2
    for j in range(eps):
        ex = e * eps + j
        start = meta_ref[i * 2 * N_EXPERTS + ex]
        cnt = meta_ref[i * 2 * N_EXPERTS + N_EXPERTS + ex]

        def block(b, carry, j=j, start=start, cnt=cnt):
            r0 = pl.multiple_of((start + b * c) * nq, nq)
            ws = read_rows(slot_ref, r0, c)
            xb = words_to_f32(ws).astype(BF16)
            gate = _dot(xb, wg_ref[j])
            hid = (gate * _sigmoid(gate)) * _dot(xb, wu_ref[j])
            yb = _dot(hid.astype(BF16), wd_ref[j]).astype(BF16).astype(F32)
            yw = ((lax.bitcast_convert_type(yb[:, :half], U32) >> 16)
                  | (lax.bitcast_convert_type(yb[:, half:], U32) & jnp.uint32(0xFFFF0000)))
            keep = lax.broadcasted_iota(jnp.int32, (c, 1), 0) < (cnt - b * c)
            for s in range(nq):
                slot_ref[pl.ds(r0 + s, c, stride=nq), :] = jnp.where(
                    keep, yw[:, s * LANES:(s + 1) * LANES], ws[s])
            return carry
        lax.fori_loop(0, (cnt + c - 1) // c, block, 0)

    @pl.when(e == pl.num_programs(1) - 1)
    def _():
        rc = min(tm, 256)
        for k in range(2):
            def comb(j, carry):
                for u in range(8):
                    t = j * 8 + u
                    p = pos_ref[pbase + 2 * t + k]
                    xt_ref[pl.ds(pl.multiple_of(t * nq, nq), nq), :] = (
                        slot_ref[pl.ds(pl.multiple_of(p * nq, nq), nq), :])
                return carry
            lax.fori_loop(0, tm // 8, comb, 0)
            for r in range(0, tm, rc):
                rows = slice(r, r + rc)
                term = words_to_f32(read_rows(xt_ref, r * nq, rc)) * w2_ref[rows, k:k + 1]
                if k == 0:
                    o_ref[rows, :] = term
                else:
                    gate = gt_ref[...] if gt_ref.shape[0] == 1 else gt_ref[rows, :]
                    xn = x_ref[rows, :] + gate * (o_ref[rows, :] + term)
                    if final:
                        xn = xn * lax.rsqrt(jnp.mean(xn * xn, axis=-1, keepdims=True) + EPS) * fg_ref[...]
                    o_ref[rows, :] = xn


def _moe_call(pos, meta, h2, x, w2, gt, fg, wg, wu, wd, tm, c, eps, per_seq_tiles, final):
    n, d = x.shape
    nt = n // tm
    if per_seq_tiles is None:
        gt_spec = pl.BlockSpec((tm, d), lambda i, e, *_: (i, 0))
    else:
        gt_spec = pl.BlockSpec((None, 1, d), lambda i, e, *_: (i // per_seq_tiles, 0, 0))
    grid_spec = pltpu.PrefetchScalarGridSpec(
        num_scalar_prefetch=2,
        grid=(nt, N_EXPERTS // eps),
        in_specs=[
            pl.BlockSpec((tm, d // 2), lambda i, e, *_: (i, 0), pipeline_mode=pl.Buffered(1)),
            pl.BlockSpec((tm, d), lambda i, e, *_: (i, 0), pipeline_mode=pl.Buffered(1)),
            pl.BlockSpec((tm, ROUTE_W), lambda i, e, *_: (i, 0)),
            gt_spec,
            pl.BlockSpec((1, d), lambda i, e, *_: (0, 0)),
            pl.BlockSpec((eps, d, MOE_HIDDEN), lambda i, e, *_: (e, 0, 0)),
            pl.BlockSpec((eps, d, MOE_HIDDEN), lambda i, e, *_: (e, 0, 0)),
            pl.BlockSpec((eps, MOE_HIDDEN, d), lambda i, e, *_: (e, 0, 0)),
        ],
        out_specs=pl.BlockSpec((tm, d), lambda i, e, *_: (i, 0)),
        scratch_shapes=[
            pltpu.VMEM((tm * (d // 2 // LANES), LANES), U32),
            pltpu.VMEM(((2 * tm + c) * (d // 2 // LANES), LANES), U32),
        ],
    )
    return pl.pallas_call(
        functools.partial(_moe_kernel, tm=tm, c=c, eps=eps, final=final),
        grid_spec=grid_spec,
        out_shape=jax.ShapeDtypeStruct((n, d), F32),
        compiler_params=_cparams(("arbitrary", "arbitrary")),
        name="moe",
    )(pos, meta, h2, x, w2, gt, fg, wg, wu, wd)


def _rel_bucket(rel):
    nb = REL_BUCKETS // 2
    max_exact = nb // 2
    ret = jnp.where(rel > 0, nb, 0)
    nabs = jnp.abs(rel)
    nf = jnp.maximum(nabs, 1).astype(F32)
    large = max_exact + (jnp.log(nf / max_exact) / jnp.log(REL_MAX_DIST / max_exact)
                         * (nb - max_exact)).astype(jnp.int32)
    large = jnp.minimum(large, nb - 1)
    return ret + jnp.where(nabs < max_exact, nabs, large)


def _swa_bias(table):
    lwin = WINDOW + CHUNK
    rel = jnp.arange(lwin)[None, :] - WINDOW - jnp.arange(CHUNK)[:, None]
    onehot = (_rel_bucket(rel)[..., None] == jnp.arange(REL_BUCKETS)).astype(F32)
    b = jnp.einsum('qlb,bh->hql', onehot, table.astype(F32), precision=HIGHEST)
    b = b.reshape(SWA_KV_HEADS, SWA_GROUP * CHUNK, lwin)
    return jnp.transpose(b, (0, 2, 1))


def _layer_params(l, p):
    d = D_MODEL
    w_in = p['w_in'][l]
    o = 0
    cols = {}
    for name, width in (('qa', 512), ('ka', 512), ('va', 1024), ('ga', GLA_RANK), ('ra', 1024),
                        ('qb', 1024), ('kb', 256), ('vb', 256), ('zc', 2048)):
        cols[name] = w_in[:, o:o + width]
        o += width
    lp = {}
    lp['wa'] = jnp.concatenate([cols['qa'], cols['ka'], cols['va'], cols['ra']], axis=1).astype(BF16)
    lp['wq'] = (cols['qb'] * (SWA_HD ** -0.5)).astype(BF16)
    lp['wkv'] = jnp.concatenate([cols['kb'], cols['vb']], axis=1).astype(BF16)
    lp['wz'] = cols['zc'].astype(BF16)
    lp['wg'] = jnp.pad(cols['ga'], ((0, 0), (0, LANES - GLA_RANK))).astype(BF16)
    lp['gla_wg'] = jnp.pad(p['gla_w_gate'][l], ((0, LANES - GLA_RANK), (0, 0)))
    lp['gla_bg'] = p['gla_b_gate'][l][None, :]
    lp['gla_ng'] = p['gla_norm_g'][l][None, :]
    sink = p['swa_sink'][l].astype(F32).reshape(SWA_KV_HEADS, SWA_GROUP, 1)
    lp['sink'] = jnp.broadcast_to(sink[:, :, None, :], (SWA_KV_HEADS, SWA_GROUP, CHUNK, 1)).reshape(
        SWA_KV_HEADS, 1, SWA_GROUP * CHUNK)
    lp['ln_g'] = p['cmlp_ln_g'][l][None, :]
    lp['ln_b'] = p['cmlp_ln_b'][l][None, :]
    ws = p['cmlp_ws'][l]
    pos = jnp.arange(CMLP_CHUNK)
    mask = (pos[None, :] // CHUNK) <= (pos[:, None] // CHUNK)
    lp['cw_full'] = jnp.where(mask[None], ws, 0.0).astype(BF16)
    lp['cw_half'] = ws[:, :CHUNK, :CHUNK].astype(BF16)
    bs = p['cmlp_bs'][l]
    bs_full = jnp.repeat(bs.T, CMLP_GW, axis=1).astype(F32)
    lp['cb_full'] = bs_full
    lp['cb_half'] = bs_full[:CHUNK]
    lp['wmg'] = p['w_merge_gate'][l].astype(BF16)
    lp['bmg'] = p['b_merge_gate'][l][None, :]
    lp['wba'] = p['w_branch_a'][l].astype(BF16)
    lp['wbb'] = p['w_branch_b'][l].astype(BF16)
    lp['wbc'] = p['w_branch_c'][l].astype(BF16)
    lp['wo'] = p['w_out'][l].astype(BF16)
    wr = jnp.concatenate([p['moe_w_re'][l], p['moe_w_rg'][l]], axis=1)
    wr = jnp.pad(wr, ((0, 0), (0, ROUTE_W - wr.shape[1])))
    wrh = wr.astype(BF16)
    lp['wrh'] = wrh
    lp['wrw'] = jnp.concatenate([wrh, (wr - wrh.astype(F32)).astype(BF16)], axis=1)
    br = jnp.concatenate([p['moe_b_re'][l], p['moe_b_rg'][l]])
    lp['br'] = jnp.pad(br, (0, ROUTE_W - br.shape[0]))[None, :].astype(F32)
    lp['ewg'] = p['moe_w_gate'][l].astype(BF16)
    lp['ewu'] = p['moe_w_up'][l].astype(BF16)
    lp['ewd'] = p['moe_w_down'][l].astype(BF16)
    lp['n1'] = p['norm1_g'][l][None, :]
    lp['n2'] = p['norm2_g'][l][None, :]
    return lp


def _pick_tile(n, pref):
    t = min(n, pref)
    while n % t:
        t //= 2
    return t


def _run_group(x3, mod, lps, bias, final_g, prompt, state_gla, cache_k, cache_v):
    nseq, seqlen, d = x3.shape
    n = nseq * seqlen
    x = x3.reshape(n, d)
    depth = len(lps)
    tm = _pick_tile(seqlen, TM_PROJ) if prompt else _pick_tile(n, TM_PROJ // 2)
    tm_moe = _pick_tile(seqlen if prompt else n, TM_MOE)
    kvw = SWA_KV_HEADS * SWA_HD
    gla_out, k_out, v_out, vc_out = [], [], [], []
    for l, lp in enumerate(lps):
        m6 = mod[l].reshape(nseq, 6, d)
        if prompt:
            mods = [m6[:, j][:, None, :] for j in range(6)]
            pst = seqlen // tm
            pst_moe = seqlen // tm_moe
        else:
            mods = [jnp.broadcast_to(m6[:, j][:, None, :], (nseq, seqlen, d)).reshape(n, d) for j in range(6)]
            pst = None
            pst_moe = None
        sh1, sc1, gt1, sh2, sc2, gt2 = mods
        if prompt:
            cw, cb, nch = lp['cw_full'], lp['cb_full'], CMLP_CHUNK
        else:
            cw, cb, nch = lp['cw_half'], lp['cb_half'], seqlen
        pa, qb, kv, oc, ga, vc = _inproj_call(x, sh1, sc1, lp['n1'], lp['wa'], lp['wq'], lp['wkv'], lp['wz'],
                                              lp['wg'], lp['ln_g'], lp['ln_b'], cw, cb, tm, pst, nch, not prompt)
        if not prompt:
            vc_out.append(vc.reshape(nseq, seqlen, CMLP_WIDTH))
        tb = _pick_tile(seqlen, TB_GLA)
        s0 = None if prompt else state_gla[l]
        oa, s_new = _gla_call(pa, ga, lp['gla_wg'], lp['gla_bg'], lp['gla_ng'], s0, nseq, seqlen, tb)
        if prompt:
            tq = _pick_tile(seqlen, TQ_SWA)
            ob = _swa_call(qb, (kv, 0), (kv, 1), (kv, 0), (kv, 1), bias, lp['sink'], nseq, seqlen, tq, True)
            kv3 = kv.reshape(nseq, seqlen, 2, SWA_KV_HEADS, SWA_HD)
            k_state = kv3[:, -WINDOW:, 0]
            v_state = kv3[:, -WINDOW:, 1]
        else:
            lc = cache_k.shape[2]
            ck = cache_k[l].reshape(nseq, lc, kvw)
            cv = cache_v[l].reshape(nseq, lc, kvw)
            ob = _swa_call(qb, (kv, 0), (kv, 1), (ck, 0), (cv, 0), bias, lp['sink'], nseq, seqlen, seqlen, False)
            kv3 = kv.reshape(nseq, seqlen, 2, SWA_KV_HEADS, SWA_HD)
            k_state = jnp.concatenate([cache_k[l], kv3[:, :, 0]], axis=1)[:, -lc:]
            v_state = jnp.concatenate([cache_v[l], kv3[:, :, 1]], axis=1)[:, -lc:]
        x1, h2, lg = _merge_call(x, [sh1, sc1, gt1, sh2, sc2], lp['n1'], lp['n2'], oa, ob, oc,
                                 lp['wmg'], lp['bmg'], lp['wba'], lp['wbb'], lp['wbc'], lp['wo'],
                                 lp['wrh'], lp['wrw'], lp['br'], tm, pst)
        pos, w2, meta = _route_call(lg, tm_moe)
        meta_flat = jnp.concatenate([meta[:, 0, :N_EXPERTS], meta[:, 1, :N_EXPERTS]], axis=1).reshape(-1)
        x = _moe_call(pos[:, :2].reshape(-1), meta_flat, h2, x1, w2, gt2, final_g, lp['ewg'], lp['ewu'], lp['ewd'],
                      tm_moe, _moe_block_rows(tm_moe), EPS_MOE, pst_moe, l == depth - 1)
        gla_out.append(s_new)
        k_out.append(k_state)
        v_out.append(v_state)
    y = x.reshape(nseq, seqlen, d)
    vc_stack = jnp.stack(vc_out) if vc_out else None
    return y, jnp.stack(gla_out), jnp.stack(k_out), jnp.stack(v_out), vc_stack


def kernel(x_prompt, x_sample, state_gla, cache_swa_k, cache_swa_v, c_prompt, c_sample, rel_table, final_norm_g, norm1_g, norm2_g, w_mod, b_mod, w_in, gla_w_gate, gla_b_gate, gla_norm_g, swa_sink, cmlp_ln_g, cmlp_ln_b, cmlp_ws, cmlp_bs, w_branch_a, w_branch_b, w_branch_c, w_merge_gate, b_merge_gate, w_out, moe_w_rg, moe_b_rg, moe_w_re, moe_b_re, moe_w_gate, moe_w_up, moe_w_down):
    p = dict(norm1_g=norm1_g, norm2_g=norm2_g, w_in=w_in, gla_w_gate=gla_w_gate, gla_b_gate=gla_b_gate,
             gla_norm_g=gla_norm_g, swa_sink=swa_sink, cmlp_ln_g=cmlp_ln_g, cmlp_ln_b=cmlp_ln_b,
             cmlp_ws=cmlp_ws, cmlp_bs=cmlp_bs, w_branch_a=w_branch_a, w_branch_b=w_branch_b,
             w_branch_c=w_branch_c, w_merge_gate=w_merge_gate, b_merge_gate=b_merge_gate, w_out=w_out,
             moe_w_rg=moe_w_rg, moe_b_rg=moe_b_rg, moe_w_re=moe_w_re, moe_b_re=moe_b_re,
             moe_w_gate=moe_w_gate, moe_w_up=moe_w_up, moe_w_down=moe_w_down)
    depth = w_in.shape[0]
    nb = c_prompt.shape[0]
    lps = [_layer_params(l, p) for l in range(depth)]
    bias = _swa_bias(rel_table)
    mod = _mod_call(jnp.concatenate([c_prompt, c_sample], axis=0), w_mod, b_mod)
    fg = final_norm_g[None, :]
    y_p, gla_p, k_p, v_p, _ = _run_group(x_prompt, mod[:, :nb], lps, bias, fg, True, None, None, None)
    y_s, gla_s, k_s, v_s, vc_s = _run_group(x_sample, mod[:, nb:], lps, bias, fg, False,
                                            state_gla, cache_swa_k, cache_swa_v)
    return (y_p, y_s, gla_p, k_p, v_p, gla_s, k_s, v_s, vc_s)
```

```python
import functools

import jax
import jax.numpy as jnp
from jax import lax
from jax.experimental import pallas as pl
from jax.experimental.pallas import tpu as pltpu

F32 = jnp.float32
BF16 = jnp.bfloat16
U32 = jnp.uint32
HIGHEST = lax.Precision.HIGHEST

D_MODEL = 1024
CHUNK = 64
EPS = 1e-6
GLA_HEADS = 4
GLA_DK = 128
GLA_DV = 256
GLA_RANK = 16
GLA_TAU = 16.0
SWA_HEADS = 16
SWA_KV_HEADS = 4
SWA_GROUP = 4
SWA_HD = 64
WINDOW = 128
REL_BUCKETS = 32
REL_MAX_DIST = 128
CMLP_CHUNK = 128
CMLP_GROUPS = 4
CMLP_WIDTH = 1024
CMLP_GW = 256
MOE_GROUPS = 4
MOE_PER_GROUP = 8
N_EXPERTS = 32
MOE_HIDDEN = 256

LANES = 128
SUBLANES = 8
VMEM_LIMIT = 62 * 1024 * 1024

TM_PROJ = 512
TB_GLA = 2048
TQ_SWA = 512
TM_MOE = 2048
EPS_MOE = 4


def _moe_block_rows(tm):
    return max(16, -(-(tm * 5 // 64) // 16) * 16)
ROUTE_W = 128


def _cparams(sem):
    return pltpu.CompilerParams(dimension_semantics=sem, vmem_limit_bytes=VMEM_LIMIT)


def _dot(a, b):
    return jnp.dot(a, b, preferred_element_type=F32)


def _dot_nt(a, b):
    return lax.dot_general(a, b, (((1,), (1,)), ((), ())), preferred_element_type=F32)


def _dot_tn(a, b):
    return lax.dot_general(a, b, (((0,), (0,)), ((), ())), preferred_element_type=F32)


def _sigmoid(x):
    return 1.0 / (1.0 + jnp.exp(-x))


def _modulate(x, g, shift, scale):
    ms = jnp.mean(x * x, axis=-1, keepdims=True)
    return x * lax.rsqrt(ms + EPS) * g * (1.0 + scale) + shift


def _mod_kernel(c_ref, w_ref, b_ref, o_ref):
    c = c_ref[...]
    s = c * _sigmoid(c)
    o_ref[0] = jnp.dot(s, w_ref[0], precision=HIGHEST, preferred_element_type=F32) + b_ref[0]


def _mod_call(c_all, w_mod, b_mod):
    depth, d, n = w_mod.shape
    r = c_all.shape[0]
    bn = 1024
    return pl.pallas_call(
        _mod_kernel,
        grid=(depth, n // bn),
        in_specs=[
            pl.BlockSpec((r, d), lambda l, j: (0, 0)),
            pl.BlockSpec((1, d, bn), lambda l, j: (l, 0, j)),
            pl.BlockSpec((1, 1, bn), lambda l, j: (l, 0, j)),
        ],
        out_specs=pl.BlockSpec((1, r, bn), lambda l, j: (l, 0, j)),
        out_shape=jax.ShapeDtypeStruct((depth, r, n), F32),
        compiler_params=_cparams(("arbitrary", "arbitrary")),
        name="mod",
    )(c_all, w_mod, b_mod.reshape(depth, 1, n))


def _inproj_kernel(*refs, n, emit_v):
    if emit_v:
        (x_ref, sh_ref, sc_ref, g_ref, wa_ref, wq_ref, wkv_ref, wz_ref, wg_ref, lg_ref, lb_ref, cw_ref, cb_ref,
         pa_ref, qb_ref, kv_ref, oc_ref, ga_ref, v_ref) = refs
    else:
        (x_ref, sh_ref, sc_ref, g_ref, wa_ref, wq_ref, wkv_ref, wz_ref, wg_ref, lg_ref, lb_ref, cw_ref, cb_ref,
         pa_ref, qb_ref, kv_ref, oc_ref, ga_ref) = refs
    tm = x_ref.shape[0]
    h = _modulate(x_ref[...], g_ref[...], sh_ref[...], sc_ref[...]).astype(BF16)
    col = 512

    def proj(w_ref, o_ref):
        for j in range(0, w_ref.shape[1], col):
            o_ref[:, j:j + col] = _dot(h, w_ref[:, j:j + col]).astype(o_ref.dtype)

    zs = []
    for j in range(0, 2 * CMLP_WIDTH, col):
        z = _dot(h, wz_ref[:, j:j + col])
        zs.append(0.5 * z * (1.0 + jnp.tanh(0.7978845608028654 * (z + 0.044715 * (z * z * z)))))
    u = jnp.concatenate(zs[:len(zs) // 2], axis=1)
    v = jnp.concatenate(zs[len(zs) // 2:], axis=1)
    proj(wa_ref, pa_ref)
    xc = v - jnp.mean(v, axis=-1, keepdims=True)
    var = jnp.mean(xc * xc, axis=-1, keepdims=True)
    vn = xc * lax.rsqrt(var + EPS) * lg_ref[...] + lb_ref[...]
    if emit_v:
        v_ref[...] = vn
    vb = vn.astype(BF16)
    for ch in range(tm // n):
        rs = slice(ch * n, (ch + 1) * n)
        for g in range(CMLP_GROUPS):
            cs = slice(g * CMLP_GW, (g + 1) * CMLP_GW)
            mix = _dot(cw_ref[g], vb[rs, cs]) + cb_ref[:, cs]
            oc_ref[rs, cs] = (u[rs, cs] * mix).astype(oc_ref.dtype)
    proj(wq_ref, qb_ref)
    proj(wkv_ref, kv_ref)
    ga_ref[...] = _dot(h, wg_ref[...])


def _row_spec(tm, width, per_seq_tiles):
    if per_seq_tiles is None:
        return pl.BlockSpec((tm, width), lambda i: (i, 0))
    return pl.BlockSpec((None, 1, width), lambda i: (i // per_seq_tiles, 0, 0))


def _const_spec(shape):
    nd = len(shape)
    return pl.BlockSpec(shape, lambda i: (0,) * nd, pipeline_mode=pl.Buffered(1))


def _inproj_call(x, sh, sc, g, wa, wq, wkv, wz, wg, lg, lb, cw, cb, tm, per_seq_tiles, n, emit_v):
    ntok, d = x.shape
    outs = [
        jax.ShapeDtypeStruct((ntok, wa.shape[1]), BF16),
        jax.ShapeDtypeStruct((ntok, wq.shape[1]), BF16),
        jax.ShapeDtypeStruct((ntok, wkv.shape[1]), F32),
        jax.ShapeDtypeStruct((ntok, CMLP_WIDTH), BF16),
        jax.ShapeDtypeStruct((ntok, LANES), F32),
    ]
    if emit_v:
        outs.append(jax.ShapeDtypeStruct((ntok, CMLP_WIDTH), F32))
    res = pl.pallas_call(
        functools.partial(_inproj_kernel, n=n, emit_v=emit_v),
        grid=(ntok // tm,),
        in_specs=[
            pl.BlockSpec((tm, d), lambda i: (i, 0)),
            _row_spec(tm, d, per_seq_tiles),
            _row_spec(tm, d, per_seq_tiles),
            _const_spec((1, d)),
            _const_spec(wa.shape), _const_spec(wq.shape), _const_spec(wkv.shape),
            _const_spec(wz.shape), _const_spec(wg.shape),
            _const_spec((1, CMLP_WIDTH)), _const_spec((1, CMLP_WIDTH)),
            _const_spec(cw.shape), _const_spec(cb.shape),
        ],
        out_specs=[pl.BlockSpec((tm, o.shape[1]), lambda i: (i, 0)) for o in outs],
        out_shape=outs,
        compiler_params=_cparams(("arbitrary",)),
        name="inproj",
    )(x, sh, sc, g, wa, wq, wkv, wz, wg, lg, lb, cw, cb)
    return res if emit_v else (*res, None)


def _gla_kernel(*refs, nblk, has_state):
    if has_state:
        (q_ref, k_ref, v_ref, r_ref, ga_ref, wg_ref, bg_ref, ng_ref, s0_ref,
         o_ref, sfin_ref, st_ref) = refs
    else:
        (q_ref, k_ref, v_ref, r_ref, ga_ref, wg_ref, bg_ref, ng_ref,
         o_ref, sfin_ref, st_ref) = refs
        s0_ref = None
    t = pl.program_id(2)

    @pl.when(t == 0)
    def _():
        if has_state:
            st_ref[...] = s0_ref[0, 0].T
        else:
            st_ref[...] = jnp.zeros_like(st_ref)

    row = lax.broadcasted_iota(jnp.int32, (CHUNK, CHUNK), 0)
    colm = lax.broadcasted_iota(jnp.int32, (CHUNK, CHUNK), 1)
    causal = row >= colm
    tril = causal.astype(F32)
    gl = _dot(ga_ref[...].astype(BF16), wg_ref[...].astype(BF16)) + bg_ref[...]
    la = (jnp.minimum(gl, 0.0) - jnp.log(1.0 + jnp.exp(-jnp.abs(gl)))) / GLA_TAU
    sls = [slice(c * CHUNK, (c + 1) * CHUNK) for c in range(nblk)]
    cum = [jnp.dot(tril, la[sl], precision=HIGHEST, preferred_element_type=F32) for sl in sls]
    qts, kts, kds, decs = [], [], [], []
    for c, sl in enumerate(sls):
        b = cum[c]
        bl = b[CHUNK - 1:CHUNK, :]
        q = q_ref[sl, :].astype(F32) * (GLA_DK ** -0.5)
        k = k_ref[sl, :].astype(F32)
        qts.append((q * jnp.exp(b)).astype(BF16))
        kts.append((k * jnp.exp(-b)).astype(BF16))
        kds.append((k * jnp.exp(bl - b)).astype(BF16))
        decs.append(jnp.exp(bl))
    amats = [jnp.where(causal, _dot_nt(qts[c], kts[c]), 0.0).astype(BF16) for c in range(nblk)]
    kvs = [_dot_tn(v_ref[sls[c], :], kds[c]) for c in range(nblk)]
    st = st_ref[...]
    states = []
    for c in range(nblk):
        states.append(st.astype(BF16))
        st = st * decs[c] + kvs[c]
    st_ref[...] = st
    for c, sl in enumerate(sls):
        o = _dot(amats[c], v_ref[sl, :]) + _dot_nt(qts[c], states[c])
        on = o * lax.rsqrt(jnp.mean(o * o, axis=-1, keepdims=True) + EPS) * ng_ref[...]
        r = r_ref[sl, :].astype(F32)
        o_ref[sl, :] = (on * (r * _sigmoid(r))).astype(o_ref.dtype)

    @pl.when(t == pl.num_programs(2) - 1)
    def _():
        sfin_ref[0, 0] = st_ref[...].T


def _gla_call(pa, ga, wg, bg, ng, s0, nseq, seqlen, tb):
    n = pa.shape[0]
    nt = seqlen // tb
    kq = GLA_HEADS
    vq = (2 * GLA_HEADS * GLA_DK) // GLA_DV
    rq = vq + GLA_HEADS
    in_specs = [
        pl.BlockSpec((tb, GLA_DK), lambda b, h, t: (b * nt + t, h)),
        pl.BlockSpec((tb, GLA_DK), lambda b, h, t: (b * nt + t, kq + h)),
        pl.BlockSpec((tb, GLA_DV), lambda b, h, t: (b * nt + t, vq + h)),
        pl.BlockSpec((tb, GLA_DV), lambda b, h, t: (b * nt + t, rq + h)),
        pl.BlockSpec((tb, LANES), lambda b, h, t: (b * nt + t, 0)),
        pl.BlockSpec((LANES, GLA_DK), lambda b, h, t: (0, h)),
        pl.BlockSpec((1, GLA_DK), lambda b, h, t: (0, h)),
        pl.BlockSpec((1, GLA_DV), lambda b, h, t: (0, 0)),
    ]
    args = [pa, pa, pa, pa, ga, wg, bg, ng]
    if s0 is not None:
        in_specs.append(pl.BlockSpec((1, 1, GLA_DK, GLA_DV), lambda b, h, t: (b, h, 0, 0)))
        args.append(s0)
    return pl.pallas_call(
        functools.partial(_gla_kernel, nblk=tb // CHUNK, has_state=s0 is not None),
        grid=(nseq, GLA_HEADS, nt),
        in_specs=in_specs,
        out_specs=[
            pl.BlockSpec((tb, GLA_DV), lambda b, h, t: (b * nt + t, h)),
            pl.BlockSpec((1, 1, GLA_DK, GLA_DV), lambda b, h, t: (b, h, 0, 0)),
        ],
        out_shape=(
            jax.ShapeDtypeStruct((n, GLA_HEADS * GLA_DV), BF16),
            jax.ShapeDtypeStruct((nseq, GLA_HEADS, GLA_DK, GLA_DV), F32),
        ),
        scratch_shapes=[pltpu.VMEM((GLA_DV, GLA_DK), F32)],
        compiler_params=_cparams(("arbitrary", "arbitrary", "arbitrary")),
        name="gla",
    )(*args)


def _swa_kernel(q_ref, kc_ref, vc_ref, kp_ref, vp_ref, bias_ref, sink_ref, o_ref, *, tq, mask_start):
    blk = pl.program_id(1)
    kall = jnp.concatenate([kp_ref[...], kc_ref[...]], axis=0).astype(BF16)
    vall = jnp.concatenate([vp_ref[...], vc_ref[...]], axis=0).astype(BF16)
    lwin = WINDOW + CHUNK
    cols = SWA_GROUP * CHUNK
    for c in range(tq // CHUNK):
        need_mask = mask_start and c * CHUNK < WINDOW
        if need_mask:
            first_pos = blk * tq + c * CHUNK - WINDOW
            valid = lax.broadcasted_iota(jnp.int32, (lwin, cols), 0) + first_pos >= 0
        scores = []
        for x in range(SWA_KV_HEADS):
            kw = kall[c * CHUNK:c * CHUNK + lwin, x * SWA_HD:(x + 1) * SWA_HD]
            base = x * SWA_GROUP * SWA_HD
            qs = jnp.concatenate(
                [q_ref[c * CHUNK:(c + 1) * CHUNK, base + g * SWA_HD:base + (g + 1) * SWA_HD]
                 for g in range(SWA_GROUP)], axis=0)
            scores.append(_dot_nt(kw, qs))
        probs = []
        for x in range(SWA_KV_HEADS):
            s = scores[x] + bias_ref[x]
            if need_mask:
                s = jnp.where(valid, s, -1e30)
            sink = sink_ref[x]
            m = jnp.maximum(jnp.max(s, axis=0, keepdims=True), sink)
            p = jnp.exp(s - m)
            den = jnp.sum(p, axis=0, keepdims=True) + jnp.exp(sink - m)
            probs.append((p.astype(BF16), 1.0 / den))
        for x in range(SWA_KV_HEADS):
            vw = vall[c * CHUNK:c * CHUNK + lwin, x * SWA_HD:(x + 1) * SWA_HD]
            base = x * SWA_GROUP * SWA_HD
            vpad = jnp.concatenate([vw, jnp.zeros_like(vw)], axis=1)
            p, rden = probs[x]
            o = (_dot_tn(vpad, p) * rden).T
            o_ref[c * CHUNK:(c + 1) * CHUNK, base:base + SWA_GROUP * SWA_HD] = jnp.concatenate(
                [o[g * CHUNK:(g + 1) * CHUNK, :SWA_HD] for g in range(SWA_GROUP)], axis=1).astype(o_ref.dtype)


def _swa_call(qb, k_cur, v_cur, k_prev, v_prev, bias, sink, nseq, seqlen, tq, prompt):
    n = qb.shape[0]
    nt = seqlen // tq
    kvw = SWA_KV_HEADS * SWA_HD
    per_prev = tq // WINDOW if prompt else None

    def cur_spec(cb):
        return pl.BlockSpec((tq, kvw), lambda b, t: (b * nt + t, cb))

    def prev_spec(cb):
        if prompt:
            nprev = seqlen // WINDOW
            return pl.BlockSpec((WINDOW, kvw),
                                lambda b, t: (b * nprev + jnp.maximum(t * per_prev - 1, 0), cb))
        return pl.BlockSpec((None, WINDOW, kvw), lambda b, t: (b, 0, cb))

    return pl.pallas_call(
        functools.partial(_swa_kernel, tq=tq, mask_start=prompt),
        grid=(nseq, nt),
        in_specs=[
            pl.BlockSpec((tq, SWA_HEADS * SWA_HD), lambda b, t: (b * nt + t, 0)),
            cur_spec(k_cur[1]), cur_spec(v_cur[1]), prev_spec(k_prev[1]), prev_spec(v_prev[1]),
            pl.BlockSpec(bias.shape, lambda b, t: (0, 0, 0)),
            pl.BlockSpec(sink.shape, lambda b, t: (0, 0, 0)),
        ],
        out_specs=pl.BlockSpec((tq, SWA_HEADS * SWA_HD), lambda b, t: (b * nt + t, 0)),
        out_shape=jax.ShapeDtypeStruct((n, SWA_HEADS * SWA_HD), BF16),
        compiler_params=_cparams(("arbitrary", "arbitrary")),
        name="swa",
    )(qb, k_cur[0], v_cur[0], k_prev[0], v_prev[0], bias, sink)


def _merge_kernel(x_ref, sh1_ref, sc1_ref, gt1_ref, sh2_ref, sc2_ref, g1_ref, g2_ref,
                  oa_ref, ob_ref, oc_ref, wmg_ref, bmg_ref, wa_ref, wb_ref, wc_ref, wo_ref,
                  wrh_ref, wrw_ref, br_ref, xo_ref, h2_ref, lg_ref):
    x = x_ref[...]
    h = _modulate(x, g1_ref[...], sh1_ref[...], sc1_ref[...]).astype(BF16)
    d = x.shape[1]
    m = None
    for i, (o_ref, w_ref) in enumerate(((oa_ref, wa_ref), (ob_ref, wb_ref), (oc_ref, wc_ref))):
        gate = _sigmoid(_dot(h, wmg_ref[:, i * d:(i + 1) * d]) + bmg_ref[:, i * d:(i + 1) * d])
        term = gate * _dot(o_ref[...], w_ref[...])
        m = term if m is None else m + term
    xn = x + gt1_ref[...] * _dot(m.astype(BF16), wo_ref[...])
    xo_ref[...] = xn
    h2 = _modulate(xn, g2_ref[...], sh2_ref[...], sc2_ref[...])
    hi = h2.astype(BF16)
    hf = hi.astype(F32)
    lo = (h2 - hf).astype(BF16)
    h2_ref[...] = ((lax.bitcast_convert_type(hf[:, :d // 2], U32) >> 16)
                   | (lax.bitcast_convert_type(hf[:, d // 2:], U32) & jnp.uint32(0xFFFF0000)))
    both = _dot(hi, wrw_ref[...])
    lg_ref[...] = (both[:, :ROUTE_W] + both[:, ROUTE_W:] + _dot(lo, wrh_ref[...])) + br_ref[...]


def _merge_call(x, mods, g1, g2, oa, ob, oc, wmg, bmg, wa, wb, wc, wo, wrh, wrw, br, tm, per_seq_tiles):
    n, d = x.shape
    tok = lambda w: pl.BlockSpec((tm, w), lambda i: (i, 0))
    in_specs = [tok(d)] + [_row_spec(tm, d, per_seq_tiles) for _ in mods]
    in_specs += [_const_spec((1, d)), _const_spec((1, d)), tok(d), tok(d), tok(d)]
    in_specs += [_const_spec(a.shape) for a in (wmg, bmg, wa, wb, wc, wo, wrh, wrw, br)]
    return pl.pallas_call(
        _merge_kernel,
        grid=(n // tm,),
        in_specs=in_specs,
        out_specs=[tok(d), tok(d // 2), tok(ROUTE_W)],
        out_shape=(
            jax.ShapeDtypeStruct((n, d), F32),
            jax.ShapeDtypeStruct((n, d // 2), U32),
            jax.ShapeDtypeStruct((n, ROUTE_W), F32),
        ),
        compiler_params=_cparams(("arbitrary",)),
        name="merge",
    )(x, *mods, g1, g2, oa, ob, oc, wmg, bmg, wa, wb, wc, wo, wrh, wrw, br)


def _route_kernel(lg_ref, pos_ref, w_ref, meta_ref):
    lg = lg_ref[...]
    tm = lg.shape[0]
    lane = lax.broadcasted_iota(jnp.int32, lg.shape, 1)
    big = jnp.int32(1 << 20)
    ninf = -jnp.inf
    is_g = (lane >= N_EXPERTS) & (lane < N_EXPERTS + MOE_GROUPS)
    gl = jnp.where(is_g, lg, ninf)
    gmax = jnp.max(gl, axis=-1, keepdims=True)
    g_idx = jnp.min(jnp.where(gl == gmax, lane - N_EXPERTS, big), axis=-1, keepdims=True)
    g_val = 1.0 / jnp.sum(jnp.exp(gl - gmax), axis=-1, keepdims=True)
    in_grp = (lane < N_EXPERTS) & ((lane // MOE_PER_GROUP) == g_idx)
    el = jnp.where(in_grp, lg, ninf)
    v0 = jnp.max(el, axis=-1, keepdims=True)
    i0 = jnp.min(jnp.where(el == v0, lane, big), axis=-1, keepdims=True)
    el1 = jnp.where(lane == i0, ninf, el)
    v1 = jnp.max(el1, axis=-1, keepdims=True)
    i1 = jnp.min(jnp.where(el1 == v1, lane, big), axis=-1, keepdims=True)
    e10 = jnp.exp(v1 - v0)
    w0 = g_val / (1.0 + e10)
    w1 = g_val * e10 / (1.0 + e10)

    hit0 = lane == i0
    hit1 = lane == i1
    assign = (hit0 | hit1)
    ab = assign.astype(BF16)
    sb = min(tm, 256)
    r_i = lax.broadcasted_iota(jnp.int32, (sb, sb), 0)
    c_i = lax.broadcasted_iota(jnp.int32, (sb, sb), 1)
    before = (c_i < r_i).astype(BF16)
    ranks = []
    cnt = jnp.zeros((1, lg.shape[1]), F32)
    for q in range(tm // sb):
        a_q = ab[q * sb:(q + 1) * sb, :]
        ranks.append(_dot(before, a_q) + cnt)
        cnt = cnt + jnp.sum(a_q.astype(F32), axis=0, keepdims=True)
    rank = jnp.concatenate(ranks, axis=0)
    cnt8 = jnp.broadcast_to(cnt, (SUBLANES, lg.shape[1]))
    inc = cnt8
    lane8 = lax.broadcasted_iota(jnp.int32, inc.shape, 1)
    sh = 1
    while sh < N_EXPERTS:
        inc = inc + jnp.where(lane8 >= sh, pltpu.roll(inc, sh, 1), 0.0)
        sh *= 2
    off8 = inc - cnt8
    slot = rank + off8[0:1, :]
    p0 = jnp.sum(jnp.where(hit0, slot, 0.0), axis=-1, keepdims=True)
    p1 = jnp.sum(jnp.where(hit1, slot, 0.0), axis=-1, keepdims=True)
    pos_ref[...] = jnp.where(lane == 0, p0, jnp.where(lane == 1, p1, 0.0)).astype(jnp.int32)
    w_ref[...] = jnp.where(lane == 0, w0, jnp.where(lane == 1, w1, 0.0))
    row8 = lax.broadcasted_iota(jnp.int32, inc.shape, 0)
    meta_ref[0] = jnp.where(row8 == 0, off8, cnt8).astype(jnp.int32)


def _route_call(lg, tm):
    n = lg.shape[0]
    nt = n // tm
    return pl.pallas_call(
        _route_kernel,
        grid=(nt,),
        in_specs=[pl.BlockSpec((tm, ROUTE_W), lambda i: (i, 0))],
        out_specs=[
            pl.BlockSpec((tm, ROUTE_W), lambda i: (i, 0)),
            pl.BlockSpec((tm, ROUTE_W), lambda i: (i, 0)),
            pl.BlockSpec((1, SUBLANES, ROUTE_W), lambda i: (i, 0, 0)),
        ],
        out_shape=(
            jax.ShapeDtypeStruct((n, ROUTE_W), jnp.int32),
            jax.ShapeDtypeStruct((n, ROUTE_W), F32),
            jax.ShapeDtypeStruct((nt, SUBLANES, ROUTE_W), jnp.int32),
        ),
        compiler_params=_cparams(("arbitrary",)),
        name="route",
    )(lg)


def _moe_kernel(pos_ref, meta_ref, h_ref, x_ref, w2_ref, gt_ref, fg_ref, wg_ref, wu_ref, wd_ref,
                o_ref, xt_ref, ct0_ref, ct1_ref, slot0_ref, slot1_ref, *, tm, c, eps, nt, final):
    g = pl.program_id(0)
    e = pl.program_id(1)
    nq = D_MODEL // 2 // LANES
    half = D_MODEL // 2
    nsteps = N_EXPERTS // eps
    half_steps = nsteps // 2
    rows_step = tm // half_steps
    per_exp = rows_step // eps
    rc = min(per_exp, 256)
    nslot = 2 * tm
    blocks_on = jnp.logical_and(g >= 1, g <= nt).astype(jnp.int32)
    mbase = jnp.clip(g - 1, 0, nt - 1) * (2 * N_EXPERTS)
    pcomb = jnp.clip(g - 2, 0, nt - 1) * nslot
    pdisp = jnp.minimum(g, nt - 1) * nslot

    def read_rows(ref, r0, nrows):
        return [ref[pl.ds(r0 + s, nrows, stride=nq), :] for s in range(nq)]

    def words_to_f32(ws):
        los = [lax.bitcast_convert_type(w << 16, F32) for w in ws]
        his = [lax.bitcast_convert_type(w & jnp.uint32(0xFFFF0000), F32) for w in ws]
        return jnp.concatenate(los + his, axis=1)

    @pl.when(jnp.logical_and(g == 0, e == 0))
    def _():
        slot0_ref[...] = jnp.zeros(slot0_ref.shape, U32)
        slot1_ref[...] = jnp.zeros(slot1_ref.shape, U32)

    def stage(cur_ref, oth_ref):
        def block(b, j, start, cnt):
            r0 = pl.multiple_of((start + b * c) * nq, nq)
            ws = read_rows(cur_ref, r0, c)
            xb = words_to_f32(ws).astype(BF16)
            gate = _dot(xb, wg_ref[j])
            hid = (gate * _sigmoid(gate)) * _dot(xb, wu_ref[j])
            yb = _dot(hid.astype(BF16), wd_ref[j]).astype(BF16).astype(F32)
            yw = ((lax.bitcast_convert_type(yb[:, :half], U32) >> 16)
                  | (lax.bitcast_convert_type(yb[:, half:], U32) & jnp.uint32(0xFFFF0000)))
            keep = lax.broadcasted_iota(jnp.int32, (c, 1), 0) < (cnt - b * c)
            for s in range(nq):
                cur_ref[pl.ds(r0 + s, c, stride=nq), :] = jnp.where(
                    keep, yw[:, s * LANES:(s + 1) * LANES], ws[s])

        def expert(j, copies):
            ex = e * eps + j
            start = meta_ref[mbase + ex]
            cnt = meta_ref[mbase + N_EXPERTS + ex] * blocks_on
            block(0, j, start, cnt)
            copies(j)

            def more(b, carry):
                block(b, j, start, cnt)
                return carry
            lax.fori_loop(1, (cnt + c - 1) // c, more, 0)

        @pl.when(e < half_steps)
        def _():
            tok0 = e * rows_step

            def gather(j):
                lo = j * per_exp
                for tl in range(lo, lo + per_exp):
                    for k, ct_ref in ((0, ct0_ref), (1, ct1_ref)):
                        p = pos_ref[pcomb + 2 * (tok0 + tl) + k]
                        ct_ref[pl.ds(tl * nq, nq), :] = oth_ref[pl.ds(pl.multiple_of(p * nq, nq), nq), :]
                for r in range(lo, lo + per_exp, rc):
                    rows = slice(r, r + rc)
                    wts = w2_ref[rows, :]
                    acc = (words_to_f32(read_rows(ct0_ref, r * nq, rc)) * wts[:, 0:1]
                           + words_to_f32(read_rows(ct1_ref, r * nq, rc)) * wts[:, 1:2])
                    gate = gt_ref[...] if gt_ref.shape[0] == 1 else gt_ref[rows, :]
                    xn = x_ref[rows, :] + gate * acc
                    if final:
                        xn = xn * lax.rsqrt(jnp.mean(xn * xn, axis=-1, keepdims=True) + EPS) * fg_ref[...]
                    o_ref[rows, :] = xn
            for j in range(eps):
                expert(j, gather)

        @pl.when(e >= half_steps)
        def _():
            @pl.when(e == half_steps)
            def _():
                for s in range(nq):
                    xt_ref[pl.ds(s, tm, stride=nq), :] = h_ref[:, s * LANES:(s + 1) * LANES]
                oth_ref[pl.ds(nslot * nq, c * nq), :] = jnp.zeros((c * nq, LANES), U32)
            tok0 = (e - half_steps) * rows_step

            def scatter(j):
                for tl in range(j * per_exp, (j + 1) * per_exp):
                    t = tok0 + tl
                    row = xt_ref[pl.ds(pl.multiple_of(t * nq, nq), nq), :]
                    for k in range(2):
                        p = pos_ref[pdisp + 2 * t + k]
                        oth_ref[pl.ds(pl.multiple_of(p * nq, nq), nq), :] = row
            for j in range(eps):
                expert(j, scatter)

    @pl.when(g % 2 == 1)
    def _():
        stage(slot0_ref, slot1_ref)

    @pl.when(g % 2 == 0)
    def _():
        stage(slot1_ref, slot0_ref)


def _moe_call(pos, meta, h2, x, w2, gt, fg, wg, wu, wd, tm, c, eps, per_seq_tiles, final):
    n, d = x.shape
    nt = n // tm
    nq = d // 2 // LANES
    nsteps = N_EXPERTS // eps
    rows_step = tm // (nsteps // 2)

    half_steps = nsteps // 2

    def done(g, e):
        return jnp.where(g >= 2, jnp.minimum(g - 2, nt - 1) * half_steps + jnp.minimum(e, half_steps - 1), 0)

    def incoming(g, e):
        return jnp.clip(jnp.where(e >= half_steps - 1, g, g - 1), 0, nt - 1)

    if per_seq_tiles is None:
        gt_spec = pl.BlockSpec((rows_step, d), lambda g, e, *_: (done(g, e), 0))
    else:
        gt_spec = pl.BlockSpec((None, 1, d),
                               lambda g, e, *_: (jnp.clip(g - 2, 0, nt - 1) // per_seq_tiles, 0, 0))
    grid_spec = pltpu.PrefetchScalarGridSpec(
        num_scalar_prefetch=2,
        grid=(nt + 2, nsteps),
        in_specs=[
            pl.BlockSpec((tm, d // 2), lambda g, e, *_: (incoming(g, e), 0)),
            pl.BlockSpec((rows_step, d), lambda g, e, *_: (done(g, e), 0)),
            pl.BlockSpec((rows_step, ROUTE_W), lambda g, e, *_: (done(g, e), 0)),
            gt_spec,
            pl.BlockSpec((1, d), lambda g, e, *_: (0, 0)),
            pl.BlockSpec((eps, d, MOE_HIDDEN), lambda g, e, *_: (e, 0, 0)),
            pl.BlockSpec((eps, d, MOE_HIDDEN), lambda g, e, *_: (e, 0, 0)),
            pl.BlockSpec((eps, MOE_HIDDEN, d), lambda g, e, *_: (e, 0, 0)),
        ],
        out_specs=pl.BlockSpec((rows_step, d), lambda g, e, *_: (done(g, e), 0)),
        scratch_shapes=[
            pltpu.VMEM((tm * nq, LANES), U32),
            pltpu.VMEM((rows_step * nq, LANES), U32),
            pltpu.VMEM((rows_step * nq, LANES), U32),
            pltpu.VMEM(((2 * tm + c) * nq, LANES), U32),
            pltpu.VMEM(((2 * tm + c) * nq, LANES), U32),
        ],
    )
    return pl.pallas_call(
        functools.partial(_moe_kernel, tm=tm, c=c, eps=eps, nt=nt, final=final),
        grid_spec=grid_spec,
        out_shape=jax.ShapeDtypeStruct((n, d), F32),
        compiler_params=_cparams(("arbitrary", "arbitrary")),
        name="moe",
    )(pos, meta, h2, x, w2, gt, fg, wg, wu, wd)


def _rel_bucket(rel):
    nb = REL_BUCKETS // 2
    max_exact = nb // 2
    ret = jnp.where(rel > 0, nb, 0)
    nabs = jnp.abs(rel)
    nf = jnp.maximum(nabs, 1).astype(F32)
    large = max_exact + (jnp.log(nf / max_exact) / jnp.log(REL_MAX_DIST / max_exact)
                         * (nb - max_exact)).astype(jnp.int32)
    large = jnp.minimum(large, nb - 1)
    return ret + jnp.where(nabs < max_exact, nabs, large)


def _swa_bias(table):
    lwin = WINDOW + CHUNK
    rel = jnp.arange(lwin)[None, :] - WINDOW - jnp.arange(CHUNK)[:, None]
    onehot = (_rel_bucket(rel)[..., None] == jnp.arange(REL_BUCKETS)).astype(F32)
    b = jnp.einsum('qlb,bh->hql', onehot, table.astype(F32), precision=HIGHEST)
    b = b.reshape(SWA_KV_HEADS, SWA_GROUP * CHUNK, lwin)
    return jnp.transpose(b, (0, 2, 1))


def _layer_params(l, p):
    d = D_MODEL
    w_in = p['w_in'][l]
    o = 0
    cols = {}
    for name, width in (('qa', 512), ('ka', 512), ('va', 1024), ('ga', GLA_RANK), ('ra', 1024),
                        ('qb', 1024), ('kb', 256), ('vb', 256), ('zc', 2048)):
        cols[name] = w_in[:, o:o + width]
        o += width
    lp = {}
    lp['wa'] = jnp.concatenate([cols['qa'], cols['ka'], cols['va'], cols['ra']], axis=1).astype(BF16)
    lp['wq'] = (cols['qb'] * (SWA_HD ** -0.5)).astype(BF16)
    lp['wkv'] = jnp.concatenate([cols['kb'], cols['vb']], axis=1).astype(BF16)
    lp['wz'] = cols['zc'].astype(BF16)
    lp['wg'] = jnp.pad(cols['ga'], ((0, 0), (0, LANES - GLA_RANK))).astype(BF16)
    lp['gla_wg'] = jnp.pad(p['gla_w_gate'][l], ((0, LANES - GLA_RANK), (0, 0)))
    lp['gla_bg'] = p['gla_b_gate'][l][None, :]
    lp['gla_ng'] = p['gla_norm_g'][l][None, :]
    sink = p['swa_sink'][l].astype(F32).reshape(SWA_KV_HEADS, SWA_GROUP, 1)
    lp['sink'] = jnp.broadcast_to(sink[:, :, None, :], (SWA_KV_HEADS, SWA_GROUP, CHUNK, 1)).reshape(
        SWA_KV_HEADS, 1, SWA_GROUP * CHUNK)
    lp['ln_g'] = p['cmlp_ln_g'][l][None, :]
    lp['ln_b'] = p['cmlp_ln_b'][l][None, :]
    ws = p['cmlp_ws'][l]
    pos = jnp.arange(CMLP_CHUNK)
    mask = (pos[None, :] // CHUNK) <= (pos[:, None] // CHUNK)
    lp['cw_full'] = jnp.where(mask[None], ws, 0.0).astype(BF16)
    lp['cw_half'] = ws[:, :CHUNK, :CHUNK].astype(BF16)
    bs = p['cmlp_bs'][l]
    bs_full = jnp.repeat(bs.T, CMLP_GW, axis=1).astype(F32)
    lp['cb_full'] = bs_full
    lp['cb_half'] = bs_full[:CHUNK]
    lp['wmg'] = p['w_merge_gate'][l].astype(BF16)
    lp['bmg'] = p['b_merge_gate'][l][None, :]
    lp['wba'] = p['w_branch_a'][l].astype(BF16)
    lp['wbb'] = p['w_branch_b'][l].astype(BF16)
    lp['wbc'] = p['w_branch_c'][l].astype(BF16)
    lp['wo'] = p['w_out'][l].astype(BF16)
    wr = jnp.concatenate([p['moe_w_re'][l], p['moe_w_rg'][l]], axis=1)
    wr = jnp.pad(wr, ((0, 0), (0, ROUTE_W - wr.shape[1])))
    wrh = wr.astype(BF16)
    lp['wrh'] = wrh
    lp['wrw'] = jnp.concatenate([wrh, (wr - wrh.astype(F32)).astype(BF16)], axis=1)
    br = jnp.concatenate([p['moe_b_re'][l], p['moe_b_rg'][l]])
    lp['br'] = jnp.pad(br, (0, ROUTE_W - br.shape[0]))[None, :].astype(F32)
    lp['ewg'] = p['moe_w_gate'][l].astype(BF16)
    lp['ewu'] = p['moe_w_up'][l].astype(BF16)
    lp['ewd'] = p['moe_w_down'][l].astype(BF16)
    lp['n1'] = p['norm1_g'][l][None, :]
    lp['n2'] = p['norm2_g'][l][None, :]
    return lp


def _pick_tile(n, pref):
    t = min(n, pref)
    while n % t:
        t //= 2
    return t


def _run_group(x3, mod, lps, bias, final_g, prompt, state_gla, cache_k, cache_v):
    nseq, seqlen, d = x3.shape
    n = nseq * seqlen
    x = x3.reshape(n, d)
    depth = len(lps)
    tm = _pick_tile(seqlen, TM_PROJ) if prompt else _pick_tile(n, TM_PROJ // 2)
    tm_moe = _pick_tile(seqlen if prompt else n, TM_MOE)
    kvw = SWA_KV_HEADS * SWA_HD
    gla_out, k_out, v_out, vc_out = [], [], [], []
    for l, lp in enumerate(lps):
        m6 = mod[l].reshape(nseq, 6, d)
        if prompt:
            mods = [m6[:, j][:, None, :] for j in range(6)]
            pst = seqlen // tm
            pst_moe = seqlen // tm_moe
        else:
            mods = [jnp.broadcast_to(m6[:, j][:, None, :], (nseq, seqlen, d)).reshape(n, d) for j in range(6)]
            pst = None
            pst_moe = None
        sh1, sc1, gt1, sh2, sc2, gt2 = mods
        if prompt:
            cw, cb, nch = lp['cw_full'], lp['cb_full'], CMLP_CHUNK
        else:
            cw, cb, nch = lp['cw_half'], lp['cb_half'], seqlen
        pa, qb, kv, oc, ga, vc = _inproj_call(x, sh1, sc1, lp['n1'], lp['wa'], lp['wq'], lp['wkv'], lp['wz'],
                                              lp['wg'], lp['ln_g'], lp['ln_b'], cw, cb, tm, pst, nch, not prompt)
        if not prompt:
            vc_out.append(vc.reshape(nseq, seqlen, CMLP_WIDTH))
        tb = _pick_tile(seqlen, TB_GLA)
        s0 = None if prompt else state_gla[l]
        oa, s_new = _gla_call(pa, ga, lp['gla_wg'], lp['gla_bg'], lp['gla_ng'], s0, nseq, seqlen, tb)
        if prompt:
            tq = _pick_tile(seqlen, TQ_SWA)
            ob = _swa_call(qb, (kv, 0), (kv, 1), (kv, 0), (kv, 1), bias, lp['sink'], nseq, seqlen, tq, True)
            kv3 = kv.reshape(nseq, seqlen, 2, SWA_KV_HEADS, SWA_HD)
            k_state = kv3[:, -WINDOW:, 0]
            v_state = kv3[:, -WINDOW:, 1]
        else:
            lc = cache_k.shape[2]
            ck = cache_k[l].reshape(nseq, lc, kvw)
            cv = cache_v[l].reshape(nseq, lc, kvw)
            ob = _swa_call(qb, (kv, 0), (kv, 1), (ck, 0), (cv, 0), bias, lp['sink'], nseq, seqlen, seqlen, False)
            kv3 = kv.reshape(nseq, seqlen, 2, SWA_KV_HEADS, SWA_HD)
            k_state = jnp.concatenate([cache_k[l], kv3[:, :, 0]], axis=1)[:, -lc:]
            v_state = jnp.concatenate([cache_v[l], kv3[:, :, 1]], axis=1)[:, -lc:]
        x1, h2, lg = _merge_call(x, [sh1, sc1, gt1, sh2, sc2], lp['n1'], lp['n2'], oa, ob, oc,
                                 lp['wmg'], lp['bmg'], lp['wba'], lp['wbb'], lp['wbc'], lp['wo'],
                                 lp['wrh'], lp['wrw'], lp['br'], tm, pst)
        pos, w2, meta = _route_call(lg, tm_moe)
        meta_flat = jnp.concatenate([meta[:, 0, :N_EXPERTS], meta[:, 1, :N_EXPERTS]], axis=1).reshape(-1)
        x = _moe_call(pos[:, :2].reshape(-1), meta_flat, h2, x1, w2, gt2, final_g, lp['ewg'], lp['ewu'], lp['ewd'],
                      tm_moe, _moe_block_rows(tm_moe), EPS_MOE, pst_moe, l == depth - 1)
        gla_out.append(s_new)
        k_out.append(k_state)
        v_out.append(v_state)
    y = x.reshape(nseq, seqlen, d)
    vc_stack = jnp.stack(vc_out) if vc_out else None
    return y, jnp.stack(gla_out), jnp.stack(k_out), jnp.stack(v_out), vc_stack


def kernel(x_prompt, x_sample, state_gla, cache_swa_k, cache_swa_v, c_prompt, c_sample, rel_table, final_norm_g, norm1_g, norm2_g, w_mod, b_mod, w_in, gla_w_gate, gla_b_gate, gla_norm_g, swa_sink, cmlp_ln_g, cmlp_ln_b, cmlp_ws, cmlp_bs, w_branch_a, w_branch_b, w_branch_c, w_merge_gate, b_merge_gate, w_out, moe_w_rg, moe_b_rg, moe_w_re, moe_b_re, moe_w_gate, moe_w_up, moe_w_down):
    p = dict(norm1_g=norm1_g, norm2_g=norm2_g, w_in=w_in, gla_w_gate=gla_w_gate, gla_b_gate=gla_b_gate,
             gla_norm_g=gla_norm_g, swa_sink=swa_sink, cmlp_ln_g=cmlp_ln_g, cmlp_ln_b=cmlp_ln_b,
             cmlp_ws=cmlp_ws, cmlp_bs=cmlp_bs, w_branch_a=w_branch_a, w_branch_b=w_branch_b,
             w_branch_c=w_branch_c, w_merge_gate=w_merge_gate, b_merge_gate=b_merge_gate, w_out=w_out,
             moe_w_rg=moe_w_rg, moe_b_rg=moe_b_rg, moe_w_re=moe_w_re, moe_b_re=moe_b_re,
             moe_w_gate=moe_w_gate, moe_w_up=moe_w_up, moe_w_down=moe_w_down)
    depth = w_in.shape[0]
    nb = c_prompt.shape[0]
    lps = [_layer_params(l, p) for l in range(depth)]
    bias = _swa_bias(rel_table)
    mod = _mod_call(jnp.concatenate([c_prompt, c_sample], axis=0), w_mod, b_mod)
    fg = final_norm_g[None, :]
    y_p, gla_p, k_p, v_p, _ = _run_group(x_prompt, mod[:, :nb], lps, bias, fg, True, None, None, None)
    y_s, gla_s, k_s, v_s, vc_s = _run_group(x_sample, mod[:, nb:], lps, bias, fg, False,
                                            state_gla, cache_swa_k, cache_swa_v)
    return (y_p, y_s, gla_p, k_p, v_p, gla_s, k_s, v_s, vc_s)
```

```python
import functools

import jax
import jax.numpy as jnp
from jax import lax
from jax.experimental import pallas as pl
from jax.experimental.pallas import tpu as pltpu

F32 = jnp.float32
BF16 = jnp.bfloat16
U32 = jnp.uint32
HIGHEST = lax.Precision.HIGHEST

D_MODEL = 1024
CHUNK = 64
EPS = 1e-6
GLA_HEADS = 4
GLA_DK = 128
GLA_DV = 256
GLA_RANK = 16
GLA_TAU = 16.0
SWA_HEADS = 16
SWA_KV_HEADS = 4
SWA_GROUP = 4
SWA_HD = 64
WINDOW = 128
REL_BUCKETS = 32
REL_MAX_DIST = 128
CMLP_CHUNK = 128
CMLP_GROUPS = 4
CMLP_WIDTH = 1024
CMLP_GW = 256
MOE_GROUPS = 4
MOE_PER_GROUP = 8
N_EXPERTS = 32
MOE_HIDDEN = 256

LANES = 128
SUBLANES = 8
VMEM_LIMIT = 62 * 1024 * 1024

TM_PROJ = 512
TB_GLA = 2048
TQ_SWA = 512
TM_MOE = 2048
EPS_MOE = 4


def _moe_block_rows(tm):
    return max(16, -(-(tm * 5 // 64) // 16) * 16)
ROUTE_W = 128


def _cparams(sem):
    return pltpu.CompilerParams(dimension_semantics=sem, vmem_limit_bytes=VMEM_LIMIT)


def _dot(a, b):
    return jnp.dot(a, b, preferred_element_type=F32)


def _dot_nt(a, b):
    return lax.dot_general(a, b, (((1,), (1,)), ((), ())), preferred_element_type=F32)


def _dot_tn(a, b):
    return lax.dot_general(a, b, (((0,), (0,)), ((), ())), preferred_element_type=F32)


def _sigmoid(x):
    return 1.0 / (1.0 + jnp.exp(-x))


def _modulate(x, g, shift, scale):
    ms = jnp.mean(x * x, axis=-1, keepdims=True)
    return x * lax.rsqrt(ms + EPS) * g * (1.0 + scale) + shift


def _mod_kernel(c_ref, w_ref, b_ref, o_ref):
    c = c_ref[...]
    s = c * _sigmoid(c)
    o_ref[0] = jnp.dot(s, w_ref[0], precision=HIGHEST, preferred_element_type=F32) + b_ref[0]


def _mod_call(c_all, w_mod, b_mod):
    depth, d, n = w_mod.shape
    r = c_all.shape[0]
    bn = 2048
    return pl.pallas_call(
        _mod_kernel,
        grid=(depth, n // bn),
        in_specs=[
            pl.BlockSpec((r, d), lambda l, j: (0, 0)),
            pl.BlockSpec((1, d, bn), lambda l, j: (l, 0, j)),
            pl.BlockSpec((1, 1, bn), lambda l, j: (l, 0, j)),
        ],
        out_specs=pl.BlockSpec((1, r, bn), lambda l, j: (l, 0, j)),
        out_shape=jax.ShapeDtypeStruct((depth, r, n), F32),
        compiler_params=_cparams(("arbitrary", "arbitrary")),
        name="mod",
    )(c_all, w_mod, b_mod.reshape(depth, 1, n))


def _inproj_kernel(*refs, n, emit_v):
    if emit_v:
        (x_ref, sh_ref, sc_ref, g_ref, wa_ref, wq_ref, wkv_ref, wz_ref, wg_ref, lg_ref, lb_ref, cw_ref, cb_ref,
         pa_ref, qb_ref, kv_ref, oc_ref, ga_ref, v_ref) = refs
    else:
        (x_ref, sh_ref, sc_ref, g_ref, wa_ref, wq_ref, wkv_ref, wz_ref, wg_ref, lg_ref, lb_ref, cw_ref, cb_ref,
         pa_ref, qb_ref, kv_ref, oc_ref, ga_ref) = refs
    tm = x_ref.shape[0]
    h = _modulate(x_ref[...], g_ref[...], sh_ref[...], sc_ref[...]).astype(BF16)
    col = 512

    def proj(w_ref, o_ref):
        for j in range(0, w_ref.shape[1], col):
            o_ref[:, j:j + col] = _dot(h, w_ref[:, j:j + col]).astype(o_ref.dtype)

    zs = []
    for j in range(0, 2 * CMLP_WIDTH, col):
        z = _dot(h, wz_ref[:, j:j + col])
        zs.append(0.5 * z * (1.0 + jnp.tanh(0.7978845608028654 * (z + 0.044715 * (z * z * z)))))
    u = jnp.concatenate(zs[:len(zs) // 2], axis=1)
    v = jnp.concatenate(zs[len(zs) // 2:], axis=1)
    proj(wa_ref, pa_ref)
    xc = v - jnp.mean(v, axis=-1, keepdims=True)
    var = jnp.mean(xc * xc, axis=-1, keepdims=True)
    vn = xc * lax.rsqrt(var + EPS) * lg_ref[...] + lb_ref[...]
    if emit_v:
        v_ref[...] = vn
    vb = vn.astype(BF16)
    for ch in range(tm // n):
        rs = slice(ch * n, (ch + 1) * n)
        for g in range(CMLP_GROUPS):
            cs = slice(g * CMLP_GW, (g + 1) * CMLP_GW)
            mix = _dot(cw_ref[g], vb[rs, cs]) + cb_ref[:, cs]
            oc_ref[rs, cs] = (u[rs, cs] * mix).astype(oc_ref.dtype)
    proj(wq_ref, qb_ref)
    proj(wkv_ref, kv_ref)
    ga_ref[...] = _dot(h, wg_ref[...])


def _row_spec(tm, width, per_seq_tiles):
    if per_seq_tiles is None:
        return pl.BlockSpec((tm, width), lambda i: (i, 0))
    return pl.BlockSpec((None, 1, width), lambda i: (i // per_seq_tiles, 0, 0))


def _const_spec(shape):
    nd = len(shape)
    return pl.BlockSpec(shape, lambda i: (0,) * nd, pipeline_mode=pl.Buffered(1))


def _inproj_call(x, sh, sc, g, wa, wq, wkv, wz, wg, lg, lb, cw, cb, tm, per_seq_tiles, n, emit_v):
    ntok, d = x.shape
    outs = [
        jax.ShapeDtypeStruct((ntok, wa.shape[1]), BF16),
        jax.ShapeDtypeStruct((ntok, wq.shape[1]), BF16),
        jax.ShapeDtypeStruct((ntok, wkv.shape[1]), F32),
        jax.ShapeDtypeStruct((ntok, CMLP_WIDTH), BF16),
        jax.ShapeDtypeStruct((ntok, LANES), F32),
    ]
    if emit_v:
        outs.append(jax.ShapeDtypeStruct((ntok, CMLP_WIDTH), F32))
    res = pl.pallas_call(
        functools.partial(_inproj_kernel, n=n, emit_v=emit_v),
        grid=(ntok // tm,),
        in_specs=[
            pl.BlockSpec((tm, d), lambda i: (i, 0)),
            _row_spec(tm, d, per_seq_tiles),
            _row_spec(tm, d, per_seq_tiles),
            _const_spec((1, d)),
            _const_spec(wa.shape), _const_spec(wq.shape), _const_spec(wkv.shape),
            _const_spec(wz.shape), _const_spec(wg.shape),
            _const_spec((1, CMLP_WIDTH)), _const_spec((1, CMLP_WIDTH)),
            _const_spec(cw.shape), _const_spec(cb.shape),
        ],
        out_specs=[pl.BlockSpec((tm, o.shape[1]), lambda i: (i, 0)) for o in outs],
        out_shape=outs,
        compiler_params=_cparams(("arbitrary",)),
        name="inproj",
    )(x, sh, sc, g, wa, wq, wkv, wz, wg, lg, lb, cw, cb)
    return res if emit_v else (*res, None)


def _gla_kernel(*refs, nblk, has_state):
    if has_state:
        (q_ref, k_ref, v_ref, r_ref, ga_ref, wg_ref, bg_ref, ng_ref, s0_ref,
         o_ref, sfin_ref, st_ref) = refs
    else:
        (q_ref, k_ref, v_ref, r_ref, ga_ref, wg_ref, bg_ref, ng_ref,
         o_ref, sfin_ref, st_ref) = refs
        s0_ref = None
    t = pl.program_id(2)

    @pl.when(t == 0)
    def _():
        if has_state:
            st_ref[...] = s0_ref[0, 0].T
        else:
            st_ref[...] = jnp.zeros_like(st_ref)

    row = lax.broadcasted_iota(jnp.int32, (CHUNK, CHUNK), 0)
    colm = lax.broadcasted_iota(jnp.int32, (CHUNK, CHUNK), 1)
    causal = row >= colm
    tril = causal.astype(F32)
    gl = _dot(ga_ref[...].astype(BF16), wg_ref[...].astype(BF16)) + bg_ref[...]
    la = (jnp.minimum(gl, 0.0) - jnp.log(1.0 + jnp.exp(-jnp.abs(gl)))) / GLA_TAU
    sls = [slice(c * CHUNK, (c + 1) * CHUNK) for c in range(nblk)]
    cum = [jnp.dot(tril, la[sl], precision=HIGHEST, preferred_element_type=F32) for sl in sls]
    qts, kts, kds, decs = [], [], [], []
    for c, sl in enumerate(sls):
        b = cum[c]
        bl = b[CHUNK - 1:CHUNK, :]
        q = q_ref[sl, :].astype(F32) * (GLA_DK ** -0.5)
        k = k_ref[sl, :].astype(F32)
        qts.append((q * jnp.exp(b)).astype(BF16))
        kts.append((k * jnp.exp(-b)).astype(BF16))
        kds.append((k * jnp.exp(bl - b)).astype(BF16))
        decs.append(jnp.exp(bl))
    amats = [jnp.where(causal, _dot_nt(qts[c], kts[c]), 0.0).astype(BF16) for c in range(nblk)]
    kvs = [_dot_tn(v_ref[sls[c], :], kds[c]) for c in range(nblk)]
    st = st_ref[...]
    states = []
    for c in range(nblk):
        states.append(st.astype(BF16))
        st = st * decs[c] + kvs[c]
    st_ref[...] = st
    for c, sl in enumerate(sls):
        o = _dot(amats[c], v_ref[sl, :]) + _dot_nt(qts[c], states[c])
        on = o * lax.rsqrt(jnp.mean(o * o, axis=-1, keepdims=True) + EPS) * ng_ref[...]
        r = r_ref[sl, :].astype(F32)
        o_ref[sl, :] = (on * (r * _sigmoid(r))).astype(o_ref.dtype)

    @pl.when(t == pl.num_programs(2) - 1)
    def _():
        sfin_ref[0, 0] = st_ref[...].T


def _gla_call(pa, ga, wg, bg, ng, s0, nseq, seqlen, tb):
    n = pa.shape[0]
    nt = seqlen // tb
    kq = GLA_HEADS
    vq = (2 * GLA_HEADS * GLA_DK) // GLA_DV
    rq = vq + GLA_HEADS
    in_specs = [
        pl.BlockSpec((tb, GLA_DK), lambda b, h, t: (b * nt + t, h)),
        pl.BlockSpec((tb, GLA_DK), lambda b, h, t: (b * nt + t, kq + h)),
        pl.BlockSpec((tb, GLA_DV), lambda b, h, t: (b * nt + t, vq + h)),
        pl.BlockSpec((tb, GLA_DV), lambda b, h, t: (b * nt + t, rq + h)),
        pl.BlockSpec((tb, LANES), lambda b, h, t: (b * nt + t, 0)),
        pl.BlockSpec((LANES, GLA_DK), lambda b, h, t: (0, h)),
        pl.BlockSpec((1, GLA_DK), lambda b, h, t: (0, h)),
        pl.BlockSpec((1, GLA_DV), lambda b, h, t: (0, 0)),
    ]
    args = [pa, pa, pa, pa, ga, wg, bg, ng]
    if s0 is not None:
        in_specs.append(pl.BlockSpec((1, 1, GLA_DK, GLA_DV), lambda b, h, t: (b, h, 0, 0)))
        args.append(s0)
    return pl.pallas_call(
        functools.partial(_gla_kernel, nblk=tb // CHUNK, has_state=s0 is not None),
        grid=(nseq, GLA_HEADS, nt),
        in_specs=in_specs,
        out_specs=[
            pl.BlockSpec((tb, GLA_DV), lambda b, h, t: (b * nt + t, h)),
            pl.BlockSpec((1, 1, GLA_DK, GLA_DV), lambda b, h, t: (b, h, 0, 0)),
        ],
        out_shape=(
            jax.ShapeDtypeStruct((n, GLA_HEADS * GLA_DV), BF16),
            jax.ShapeDtypeStruct((nseq, GLA_HEADS, GLA_DK, GLA_DV), F32),
        ),
        scratch_shapes=[pltpu.VMEM((GLA_DV, GLA_DK), F32)],
        compiler_params=_cparams(("arbitrary", "arbitrary", "arbitrary")),
        name="gla",
    )(*args)


def _swa_kernel(q_ref, kc_ref, vc_ref, kp_ref, vp_ref, bias_ref, sink_ref, o_ref, *, tq, mask_start):
    blk = pl.program_id(1)
    kall = jnp.concatenate([kp_ref[...], kc_ref[...]], axis=0).astype(BF16)
    vall = jnp.concatenate([vp_ref[...], vc_ref[...]], axis=0).astype(BF16)
    lwin = WINDOW + CHUNK
    cols = SWA_GROUP * CHUNK
    for c in range(tq // CHUNK):
        need_mask = mask_start and c * CHUNK < WINDOW
        if need_mask:
            first_pos = blk * tq + c * CHUNK - WINDOW
            valid = lax.broadcasted_iota(jnp.int32, (lwin, cols), 0) + first_pos >= 0
        scores = []
        for x in range(SWA_KV_HEADS):
            kw = kall[c * CHUNK:c * CHUNK + lwin, x * SWA_HD:(x + 1) * SWA_HD]
            base = x * SWA_GROUP * SWA_HD
            qs = jnp.concatenate(
                [q_ref[c * CHUNK:(c + 1) * CHUNK, base + g * SWA_HD:base + (g + 1) * SWA_HD]
                 for g in range(SWA_GROUP)], axis=0)
            scores.append(_dot_nt(kw, qs))
        probs = []
        for x in range(SWA_KV_HEADS):
            s = scores[x] + bias_ref[x]
            if need_mask:
                s = jnp.where(valid, s, -1e30)
            sink = sink_ref[x]
            m = jnp.maximum(jnp.max(s, axis=0, keepdims=True), sink)
            p = jnp.exp(s - m)
            den = jnp.sum(p, axis=0, keepdims=True) + jnp.exp(sink - m)
            probs.append((p.astype(BF16), 1.0 / den))
        for x in range(SWA_KV_HEADS):
            vw = vall[c * CHUNK:c * CHUNK + lwin, x * SWA_HD:(x + 1) * SWA_HD]
            base = x * SWA_GROUP * SWA_HD
            vpad = jnp.concatenate([vw, jnp.zeros_like(vw)], axis=1)
            p, rden = probs[x]
            o = (_dot_tn(vpad, p) * rden).T
            o_ref[c * CHUNK:(c + 1) * CHUNK, base:base + SWA_GROUP * SWA_HD] = jnp.concatenate(
                [o[g * CHUNK:(g + 1) * CHUNK, :SWA_HD] for g in range(SWA_GROUP)], axis=1).astype(o_ref.dtype)


def _swa_call(qb, k_cur, v_cur, k_prev, v_prev, bias, sink, nseq, seqlen, tq, prompt):
    n = qb.shape[0]
    nt = seqlen // tq
    kvw = SWA_KV_HEADS * SWA_HD
    per_prev = tq // WINDOW if prompt else None

    def cur_spec(cb):
        return pl.BlockSpec((tq, kvw), lambda b, t: (b * nt + t, cb))

    def prev_spec(cb):
        if prompt:
            nprev = seqlen // WINDOW
            return pl.BlockSpec((WINDOW, kvw),
                                lambda b, t: (b * nprev + jnp.maximum(t * per_prev - 1, 0), cb))
        return pl.BlockSpec((None, WINDOW, kvw), lambda b, t: (b, 0, cb))

    return pl.pallas_call(
        functools.partial(_swa_kernel, tq=tq, mask_start=prompt),
        grid=(nseq, nt),
        in_specs=[
            pl.BlockSpec((tq, SWA_HEADS * SWA_HD), lambda b, t: (b * nt + t, 0)),
            cur_spec(k_cur[1]), cur_spec(v_cur[1]), prev_spec(k_prev[1]), prev_spec(v_prev[1]),
            pl.BlockSpec(bias.shape, lambda b, t: (0, 0, 0)),
            pl.BlockSpec(sink.shape, lambda b, t: (0, 0, 0)),
        ],
        out_specs=pl.BlockSpec((tq, SWA_HEADS * SWA_HD), lambda b, t: (b * nt + t, 0)),
        out_shape=jax.ShapeDtypeStruct((n, SWA_HEADS * SWA_HD), BF16),
        compiler_params=_cparams(("arbitrary", "arbitrary")),
        name="swa",
    )(qb, k_cur[0], v_cur[0], k_prev[0], v_prev[0], bias, sink)


def _merge_kernel(x_ref, sh1_ref, sc1_ref, gt1_ref, sh2_ref, sc2_ref, g1_ref, g2_ref,
                  oa_ref, ob_ref, oc_ref, wmg_ref, bmg_ref, wa_ref, wb_ref, wc_ref, wo_ref,
                  wrh_ref, wrw_ref, br_ref, xo_ref, h2_ref, lg_ref):
    x = x_ref[...]
    h = _modulate(x, g1_ref[...], sh1_ref[...], sc1_ref[...]).astype(BF16)
    d = x.shape[1]
    m = None
    for i, (o_ref, w_ref) in enumerate(((oa_ref, wa_ref), (ob_ref, wb_ref), (oc_ref, wc_ref))):
        gate = _sigmoid(_dot(h, wmg_ref[:, i * d:(i + 1) * d]) + bmg_ref[:, i * d:(i + 1) * d])
        term = gate * _dot(o_ref[...], w_ref[...])
        m = term if m is None else m + term
    xn = x + gt1_ref[...] * _dot(m.astype(BF16), wo_ref[...])
    xo_ref[...] = xn
    h2 = _modulate(xn, g2_ref[...], sh2_ref[...], sc2_ref[...])
    hi = h2.astype(BF16)
    hf = hi.astype(F32)
    lo = (h2 - hf).astype(BF16)
    h2_ref[...] = ((lax.bitcast_convert_type(hf[:, :d // 2], U32) >> 16)
                   | (lax.bitcast_convert_type(hf[:, d // 2:], U32) & jnp.uint32(0xFFFF0000)))
    both = _dot(hi, wrw_ref[...])
    lg_ref[...] = (both[:, :ROUTE_W] + both[:, ROUTE_W:] + _dot(lo, wrh_ref[...])) + br_ref[...]


def _merge_call(x, mods, g1, g2, oa, ob, oc, wmg, bmg, wa, wb, wc, wo, wrh, wrw, br, tm, per_seq_tiles):
    n, d = x.shape
    tok = lambda w: pl.BlockSpec((tm, w), lambda i: (i, 0))
    in_specs = [tok(d)] + [_row_spec(tm, d, per_seq_tiles) for _ in mods]
    in_specs += [_const_spec((1, d)), _const_spec((1, d)), tok(d), tok(d), tok(d)]
    in_specs += [_const_spec(a.shape) for a in (wmg, bmg, wa, wb, wc, wo, wrh, wrw, br)]
    return pl.pallas_call(
        _merge_kernel,
        grid=(n // tm,),
        in_specs=in_specs,
        out_specs=[tok(d), tok(d // 2), tok(ROUTE_W)],
        out_shape=(
            jax.ShapeDtypeStruct((n, d), F32),
            jax.ShapeDtypeStruct((n, d // 2), U32),
            jax.ShapeDtypeStruct((n, ROUTE_W), F32),
        ),
        compiler_params=_cparams(("arbitrary",)),
        name="merge",
    )(x, *mods, g1, g2, oa, ob, oc, wmg, bmg, wa, wb, wc, wo, wrh, wrw, br)


def _route_kernel(lg_ref, pos_ref, w_ref, meta_ref):
    lg = lg_ref[...]
    tm = lg.shape[0]
    lane = lax.broadcasted_iota(jnp.int32, lg.shape, 1)
    big = jnp.int32(1 << 20)
    ninf = -jnp.inf
    is_g = (lane >= N_EXPERTS) & (lane < N_EXPERTS + MOE_GROUPS)
    gl = jnp.where(is_g, lg, ninf)
    gmax = jnp.max(gl, axis=-1, keepdims=True)
    g_idx = jnp.min(jnp.where(gl == gmax, lane - N_EXPERTS, big), axis=-1, keepdims=True)
    g_val = 1.0 / jnp.sum(jnp.exp(gl - gmax), axis=-1, keepdims=True)
    in_grp = (lane < N_EXPERTS) & ((lane // MOE_PER_GROUP) == g_idx)
    el = jnp.where(in_grp, lg, ninf)
    v0 = jnp.max(el, axis=-1, keepdims=True)
    i0 = jnp.min(jnp.where(el == v0, lane, big), axis=-1, keepdims=True)
    el1 = jnp.where(lane == i0, ninf, el)
    v1 = jnp.max(el1, axis=-1, keepdims=True)
    i1 = jnp.min(jnp.where(el1 == v1, lane, big), axis=-1, keepdims=True)
    e10 = jnp.exp(v1 - v0)
    w0 = g_val / (1.0 + e10)
    w1 = g_val * e10 / (1.0 + e10)

    hit0 = lane == i0
    hit1 = lane == i1
    assign = (hit0 | hit1)
    ab = assign.astype(BF16)
    sb = min(tm, 256)
    r_i = lax.broadcasted_iota(jnp.int32, (sb, sb), 0)
    c_i = lax.broadcasted_iota(jnp.int32, (sb, sb), 1)
    before = (c_i < r_i).astype(BF16)
    ranks = []
    cnt = jnp.zeros((1, lg.shape[1]), F32)
    for q in range(tm // sb):
        a_q = ab[q * sb:(q + 1) * sb, :]
        ranks.append(_dot(before, a_q) + cnt)
        cnt = cnt + jnp.sum(a_q.astype(F32), axis=0, keepdims=True)
    rank = jnp.concatenate(ranks, axis=0)
    cnt8 = jnp.broadcast_to(cnt, (SUBLANES, lg.shape[1]))
    inc = cnt8
    lane8 = lax.broadcasted_iota(jnp.int32, inc.shape, 1)
    sh = 1
    while sh < N_EXPERTS:
        inc = inc + jnp.where(lane8 >= sh, pltpu.roll(inc, sh, 1), 0.0)
        sh *= 2
    off8 = inc - cnt8
    slot = rank + off8[0:1, :]
    p0 = jnp.sum(jnp.where(hit0, slot, 0.0), axis=-1, keepdims=True)
    p1 = jnp.sum(jnp.where(hit1, slot, 0.0), axis=-1, keepdims=True)
    pos_t = jnp.where(lane == 0, p0, jnp.where(lane == 1, p1, 0.0)).T
    pos_ref[...] = pos_t[:SUBLANES, :].astype(jnp.int32)
    w_ref[...] = jnp.where(lane == 0, w0, jnp.where(lane == 1, w1, 0.0))
    row8 = lax.broadcasted_iota(jnp.int32, inc.shape, 0)
    meta_ref[0] = jnp.where(row8 == 0, off8, cnt8).astype(jnp.int32)


def _route_call(lg, tm):
    n = lg.shape[0]
    nt = n // tm
    return pl.pallas_call(
        _route_kernel,
        grid=(nt,),
        in_specs=[pl.BlockSpec((tm, ROUTE_W), lambda i: (i, 0))],
        out_specs=[
            pl.BlockSpec((SUBLANES, tm), lambda i: (i, 0)),
            pl.BlockSpec((tm, ROUTE_W), lambda i: (i, 0)),
            pl.BlockSpec((1, SUBLANES, ROUTE_W), lambda i: (i, 0, 0)),
        ],
        out_shape=(
            jax.ShapeDtypeStruct((nt * SUBLANES, tm), jnp.int32),
            jax.ShapeDtypeStruct((n, ROUTE_W), F32),
            jax.ShapeDtypeStruct((nt, SUBLANES, ROUTE_W), jnp.int32),
        ),
        compiler_params=_cparams(("arbitrary",)),
        name="route",
    )(lg)


def _moe_kernel(pos_ref, meta_ref, h_ref, x_ref, w2_ref, gt_ref, fg_ref, wg_ref, wu_ref, wd_ref,
                o_ref, xt_ref, ct0_ref, ct1_ref, slot0_ref, slot1_ref, *, tm, c, eps, nt, final):
    g = pl.program_id(0)
    e = pl.program_id(1)
    nq = D_MODEL // 2 // LANES
    half = D_MODEL // 2
    nsteps = N_EXPERTS // eps
    half_steps = nsteps // 2
    rows_step = tm // half_steps
    per_exp = rows_step // eps
    rc = min(per_exp, 256)
    nslot = 2 * tm
    blocks_on = jnp.logical_and(g >= 1, g <= nt).astype(jnp.int32)
    mbase = jnp.clip(g - 1, 0, nt - 1) * (2 * N_EXPERTS)
    pcomb = jnp.clip(g - 2, 0, nt - 1) * nslot
    pdisp = jnp.minimum(g, nt - 1) * nslot

    def read_rows(ref, r0, nrows):
        return [ref[pl.ds(r0 + s, nrows, stride=nq), :] for s in range(nq)]

    def words_to_f32(ws):
        los = [lax.bitcast_convert_type(w << 16, F32) for w in ws]
        his = [lax.bitcast_convert_type(w & jnp.uint32(0xFFFF0000), F32) for w in ws]
        return jnp.concatenate(los + his, axis=1)

    @pl.when(jnp.logical_and(g == 0, e == 0))
    def _():
        slot0_ref[...] = jnp.zeros(slot0_ref.shape, U32)
        slot1_ref[...] = jnp.zeros(slot1_ref.shape, U32)

    def stage(cur_ref, oth_ref):
        def block(b, j, start, cnt):
            r0 = pl.multiple_of((start + b * c) * nq, nq)
            ws = read_rows(cur_ref, r0, c)
            xb = words_to_f32(ws).astype(BF16)
            gate = _dot(xb, wg_ref[j])
            hid = (gate * _sigmoid(gate)) * _dot(xb, wu_ref[j])
            yb = _dot(hid.astype(BF16), wd_ref[j]).astype(BF16).astype(F32)
            yw = ((lax.bitcast_convert_type(yb[:, :half], U32) >> 16)
                  | (lax.bitcast_convert_type(yb[:, half:], U32) & jnp.uint32(0xFFFF0000)))
            keep = lax.broadcasted_iota(jnp.int32, (c, 1), 0) < (cnt - b * c)
            for s in range(nq):
                cur_ref[pl.ds(r0 + s, c, stride=nq), :] = jnp.where(
                    keep, yw[:, s * LANES:(s + 1) * LANES], ws[s])

        def expert(j, copies):
            ex = e * eps + j
            start = meta_ref[mbase + ex]
            cnt = meta_ref[mbase + N_EXPERTS + ex] * blocks_on
            block(0, j, start, cnt)
            copies(j)

            def more(b, carry):
                block(b, j, start, cnt)
                return carry
            lax.fori_loop(1, (cnt + c - 1) // c, more, 0)

        @pl.when(e < half_steps)
        def _():
            tok0 = e * rows_step

            def gather(j):
                lo = j * per_exp
                for tl in range(lo, lo + per_exp):
                    for k, ct_ref in ((0, ct0_ref), (1, ct1_ref)):
                        p = pos_ref[pcomb + k * tm + tok0 + tl]
                        ct_ref[pl.ds(tl * nq, nq), :] = oth_ref[pl.ds(pl.multiple_of(p * nq, nq), nq), :]
                for r in range(lo, lo + per_exp, rc):
                    rows = slice(r, r + rc)
                    wts = w2_ref[rows, :]
                    acc = (words_to_f32(read_rows(ct0_ref, r * nq, rc)) * wts[:, 0:1]
                           + words_to_f32(read_rows(ct1_ref, r * nq, rc)) * wts[:, 1:2])
                    gate = gt_ref[...] if gt_ref.shape[0] == 1 else gt_ref[rows, :]
                    xn = x_ref[rows, :] + gate * acc
                    if final:
                        xn = xn * lax.rsqrt(jnp.mean(xn * xn, axis=-1, keepdims=True) + EPS) * fg_ref[...]
                    o_ref[rows, :] = xn
            for j in range(eps):
                expert(j, gather)

        @pl.when(e >= half_steps)
        def _():
            @pl.when(e == half_steps)
            def _():
                for s in range(nq):
                    xt_ref[pl.ds(s, tm, stride=nq), :] = h_ref[:, s * LANES:(s + 1) * LANES]
                oth_ref[pl.ds(nslot * nq, c * nq), :] = jnp.zeros((c * nq, LANES), U32)
            tok0 = (e - half_steps) * rows_step

            def scatter(j):
                for tl in range(j * per_exp, (j + 1) * per_exp):
                    t = tok0 + tl
                    row = xt_ref[pl.ds(pl.multiple_of(t * nq, nq), nq), :]
                    for k in range(2):
                        p = pos_ref[pdisp + k * tm + t]
                        oth_ref[pl.ds(pl.multiple_of(p * nq, nq), nq), :] = row
            for j in range(eps):
                expert(j, scatter)

    @pl.when(g % 2 == 1)
    def _():
        stage(slot0_ref, slot1_ref)

    @pl.when(g % 2 == 0)
    def _():
        stage(slot1_ref, slot0_ref)


def _moe_call(pos, meta, h2, x, w2, gt, fg, wg, wu, wd, tm, c, eps, per_seq_tiles, final):
    n, d = x.shape
    nt = n // tm
    nq = d // 2 // LANES
    nsteps = N_EXPERTS // eps
    rows_step = tm // (nsteps // 2)

    half_steps = nsteps // 2

    def done(g, e):
        return jnp.where(g >= 2, jnp.minimum(g - 2, nt - 1) * half_steps + jnp.minimum(e, half_steps - 1), 0)

    def incoming(g, e):
        return jnp.clip(jnp.where(e >= half_steps - 1, g, g - 1), 0, nt - 1)

    if per_seq_tiles is None:
        gt_spec = pl.BlockSpec((rows_step, d), lambda g, e, *_: (done(g, e), 0))
    else:
        gt_spec = pl.BlockSpec((None, 1, d),
                               lambda g, e, *_: (jnp.clip(g - 2, 0, nt - 1) // per_seq_tiles, 0, 0))
    grid_spec = pltpu.PrefetchScalarGridSpec(
        num_scalar_prefetch=2,
        grid=(nt + 2, nsteps),
        in_specs=[
            pl.BlockSpec((tm, d // 2), lambda g, e, *_: (incoming(g, e), 0)),
            pl.BlockSpec((rows_step, d), lambda g, e, *_: (done(g, e), 0)),
            pl.BlockSpec((rows_step, ROUTE_W), lambda g, e, *_: (done(g, e), 0)),
            gt_spec,
            pl.BlockSpec((1, d), lambda g, e, *_: (0, 0)),
            pl.BlockSpec((eps, d, MOE_HIDDEN), lambda g, e, *_: (e, 0, 0)),
            pl.BlockSpec((eps, d, MOE_HIDDEN), lambda g, e, *_: (e, 0, 0)),
            pl.BlockSpec((eps, MOE_HIDDEN, d), lambda g, e, *_: (e, 0, 0)),
        ],
        out_specs=pl.BlockSpec((rows_step, d), lambda g, e, *_: (done(g, e), 0)),
        scratch_shapes=[
            pltpu.VMEM((tm * nq, LANES), U32),
            pltpu.VMEM((rows_step * nq, LANES), U32),
            pltpu.VMEM((rows_step * nq, LANES), U32),
            pltpu.VMEM(((2 * tm + c) * nq, LANES), U32),
            pltpu.VMEM(((2 * tm + c) * nq, LANES), U32),
        ],
    )
    return pl.pallas_call(
        functools.partial(_moe_kernel, tm=tm, c=c, eps=eps, nt=nt, final=final),
        grid_spec=grid_spec,
        out_shape=jax.ShapeDtypeStruct((n, d), F32),
        compiler_params=_cparams(("arbitrary", "arbitrary")),
        name="moe",
    )(pos, meta, h2, x, w2, gt, fg, wg, wu, wd)


def _rel_bucket(rel):
    nb = REL_BUCKETS // 2
    max_exact = nb // 2
    ret = jnp.where(rel > 0, nb, 0)
    nabs = jnp.abs(rel)
    nf = jnp.maximum(nabs, 1).astype(F32)
    large = max_exact + (jnp.log(nf / max_exact) / jnp.log(REL_MAX_DIST / max_exact)
                         * (nb - max_exact)).astype(jnp.int32)
    large = jnp.minimum(large, nb - 1)
    return ret + jnp.where(nabs < max_exact, nabs, large)


def _swa_bias(table):
    lwin = WINDOW + CHUNK
    rel = jnp.arange(lwin)[None, :] - WINDOW - jnp.arange(CHUNK)[:, None]
    onehot = (_rel_bucket(rel)[..., None] == jnp.arange(REL_BUCKETS)).astype(F32)
    b = jnp.einsum('qlb,bh->hql', onehot, table.astype(F32), precision=HIGHEST)
    b = b.reshape(SWA_KV_HEADS, SWA_GROUP * CHUNK, lwin)
    return jnp.transpose(b, (0, 2, 1))


def _layer_params(l, p):
    d = D_MODEL
    w_in = p['w_in'][l]
    o = 0
    cols = {}
    for name, width in (('qa', 512), ('ka', 512), ('va', 1024), ('ga', GLA_RANK), ('ra', 1024),
                        ('qb', 1024), ('kb', 256), ('vb', 256), ('zc', 2048)):
        cols[name] = w_in[:, o:o + width]
        o += width
    lp = {}
    lp['wa'] = jnp.concatenate([cols['qa'], cols['ka'], cols['va'], cols['ra']], axis=1).astype(BF16)
    lp['wq'] = (cols['qb'] * (SWA_HD ** -0.5)).astype(BF16)
    lp['wkv'] = jnp.concatenate([cols['kb'], cols['vb']], axis=1).astype(BF16)
    lp['wz'] = cols['zc'].astype(BF16)
    lp['wg'] = jnp.pad(cols['ga'], ((0, 0), (0, LANES - GLA_RANK))).astype(BF16)
    lp['gla_wg'] = jnp.pad(p['gla_w_gate'][l], ((0, LANES - GLA_RANK), (0, 0)))
    lp['gla_bg'] = p['gla_b_gate'][l][None, :]
    lp['gla_ng'] = p['gla_norm_g'][l][None, :]
    sink = p['swa_sink'][l].astype(F32).reshape(SWA_KV_HEADS, SWA_GROUP, 1)
    lp['sink'] = jnp.broadcast_to(sink[:, :, None, :], (SWA_KV_HEADS, SWA_GROUP, CHUNK, 1)).reshape(
        SWA_KV_HEADS, 1, SWA_GROUP * CHUNK)
    lp['ln_g'] = p['cmlp_ln_g'][l][None, :]
    lp['ln_b'] = p['cmlp_ln_b'][l][None, :]
    ws = p['cmlp_ws'][l]
    pos = jnp.arange(CMLP_CHUNK)
    mask = (pos[None, :] // CHUNK) <= (pos[:, None] // CHUNK)
    lp['cw_full'] = jnp.where(mask[None], ws, 0.0).astype(BF16)
    lp['cw_half'] = ws[:, :CHUNK, :CHUNK].astype(BF16)
    bs = p['cmlp_bs'][l]
    bs_full = jnp.repeat(bs.T, CMLP_GW, axis=1).astype(F32)
    lp['cb_full'] = bs_full
    lp['cb_half'] = bs_full[:CHUNK]
    lp['wmg'] = p['w_merge_gate'][l].astype(BF16)
    lp['bmg'] = p['b_merge_gate'][l][None, :]
    lp['wba'] = p['w_branch_a'][l].astype(BF16)
    lp['wbb'] = p['w_branch_b'][l].astype(BF16)
    lp['wbc'] = p['w_branch_c'][l].astype(BF16)
    lp['wo'] = p['w_out'][l].astype(BF16)
    wr = jnp.concatenate([p['moe_w_re'][l], p['moe_w_rg'][l]], axis=1)
    wr = jnp.pad(wr, ((0, 0), (0, ROUTE_W - wr.shape[1])))
    wrh = wr.astype(BF16)
    lp['wrh'] = wrh
    lp['wrw'] = jnp.concatenate([wrh, (wr - wrh.astype(F32)).astype(BF16)], axis=1)
    br = jnp.concatenate([p['moe_b_re'][l], p['moe_b_rg'][l]])
    lp['br'] = jnp.pad(br, (0, ROUTE_W - br.shape[0]))[None, :].astype(F32)
    lp['ewg'] = p['moe_w_gate'][l].astype(BF16)
    lp['ewu'] = p['moe_w_up'][l].astype(BF16)
    lp['ewd'] = p['moe_w_down'][l].astype(BF16)
    lp['n1'] = p['norm1_g'][l][None, :]
    lp['n2'] = p['norm2_g'][l][None, :]
    return lp


def _pick_tile(n, pref):
    t = min(n, pref)
    while n % t:
        t //= 2
    return t


def _run_group(x3, mod, lps, bias, final_g, prompt, state_gla, cache_k, cache_v):
    nseq, seqlen, d = x3.shape
    n = nseq * seqlen
    x = x3.reshape(n, d)
    depth = len(lps)
    tm = _pick_tile(seqlen, TM_PROJ) if prompt else _pick_tile(n, TM_PROJ // 2)
    tm_moe = _pick_tile(seqlen if prompt else n, TM_MOE)
    kvw = SWA_KV_HEADS * SWA_HD
    gla_out, k_out, v_out, vc_out = [], [], [], []
    for l, lp in enumerate(lps):
        m6 = mod[l].reshape(nseq, 6, d)
        if prompt:
            mods = [m6[:, j][:, None, :] for j in range(6)]
            pst = seqlen // tm
            pst_moe = seqlen // tm_moe
        else:
            mods = [jnp.broadcast_to(m6[:, j][:, None, :], (nseq, seqlen, d)).reshape(n, d) for j in range(6)]
            pst = None
            pst_moe = None
        sh1, sc1, gt1, sh2, sc2, gt2 = mods
        if prompt:
            cw, cb, nch = lp['cw_full'], lp['cb_full'], CMLP_CHUNK
        else:
            cw, cb, nch = lp['cw_half'], lp['cb_half'], seqlen
        pa, qb, kv, oc, ga, vc = _inproj_call(x, sh1, sc1, lp['n1'], lp['wa'], lp['wq'], lp['wkv'], lp['wz'],
                                              lp['wg'], lp['ln_g'], lp['ln_b'], cw, cb, tm, pst, nch, not prompt)
        if not prompt:
            vc_out.append(vc.reshape(nseq, seqlen, CMLP_WIDTH))
        tb = _pick_tile(seqlen, TB_GLA)
        s0 = None if prompt else state_gla[l]
        oa, s_new = _gla_call(pa, ga, lp['gla_wg'], lp['gla_bg'], lp['gla_ng'], s0, nseq, seqlen, tb)
        if prompt:
            tq = _pick_tile(seqlen, TQ_SWA)
            ob = _swa_call(qb, (kv, 0), (kv, 1), (kv, 0), (kv, 1), bias, lp['sink'], nseq, seqlen, tq, True)
            kv3 = kv.reshape(nseq, seqlen, 2, SWA_KV_HEADS, SWA_HD)
            k_state = kv3[:, -WINDOW:, 0]
            v_state = kv3[:, -WINDOW:, 1]
        else:
            lc = cache_k.shape[2]
            ck = cache_k[l].reshape(nseq, lc, kvw)
            cv = cache_v[l].reshape(nseq, lc, kvw)
            ob = _swa_call(qb, (kv, 0), (kv, 1), (ck, 0), (cv, 0), bias, lp['sink'], nseq, seqlen, seqlen, False)
            kv3 = kv.reshape(nseq, seqlen, 2, SWA_KV_HEADS, SWA_HD)
            k_state = jnp.concatenate([cache_k[l], kv3[:, :, 0]], axis=1)[:, -lc:]
            v_state = jnp.concatenate([cache_v[l], kv3[:, :, 1]], axis=1)[:, -lc:]
        x1, h2, lg = _merge_call(x, [sh1, sc1, gt1, sh2, sc2], lp['n1'], lp['n2'], oa, ob, oc,
                                 lp['wmg'], lp['bmg'], lp['wba'], lp['wbb'], lp['wbc'], lp['wo'],
                                 lp['wrh'], lp['wrw'], lp['br'], tm, pst)
        pos, w2, meta = _route_call(lg, tm_moe)
        meta_flat = jnp.concatenate([meta[:, 0, :N_EXPERTS], meta[:, 1, :N_EXPERTS]], axis=1).reshape(-1)
        pos_flat = pos.reshape(-1, SUBLANES, tm_moe)[:, :2, :].reshape(-1)
        x = _moe_call(pos_flat, meta_flat, h2, x1, w2, gt2, final_g, lp['ewg'], lp['ewu'], lp['ewd'],
                      tm_moe, _moe_block_rows(tm_moe), EPS_MOE, pst_moe, l == depth - 1)
        gla_out.append(s_new)
        k_out.append(k_state)
        v_out.append(v_state)
    y = x.reshape(nseq, seqlen, d)
    vc_stack = jnp.stack(vc_out) if vc_out else None
    return y, jnp.stack(gla_out), jnp.stack(k_out), jnp.stack(v_out), vc_stack


def kernel(x_prompt, x_sample, state_gla, cache_swa_k, cache_swa_v, c_prompt, c_sample, rel_table, final_norm_g, norm1_g, norm2_g, w_mod, b_mod, w_in, gla_w_gate, gla_b_gate, gla_norm_g, swa_sink, cmlp_ln_g, cmlp_ln_b, cmlp_ws, cmlp_bs, w_branch_a, w_branch_b, w_branch_c, w_merge_gate, b_merge_gate, w_out, moe_w_rg, moe_b_rg, moe_w_re, moe_b_re, moe_w_gate, moe_w_up, moe_w_down):
    p = dict(norm1_g=norm1_g, norm2_g=norm2_g, w_in=w_in, gla_w_gate=gla_w_gate, gla_b_gate=gla_b_gate,
             gla_norm_g=gla_norm_g, swa_sink=swa_sink, cmlp_ln_g=cmlp_ln_g, cmlp_ln_b=cmlp_ln_b,
             cmlp_ws=cmlp_ws, cmlp_bs=cmlp_bs, w_branch_a=w_branch_a, w_branch_b=w_branch_b,
             w_branch_c=w_branch_c, w_merge_gate=w_merge_gate, b_merge_gate=b_merge_gate, w_out=w_out,
             moe_w_rg=moe_w_rg, moe_b_rg=moe_b_rg, moe_w_re=moe_w_re, moe_b_re=moe_b_re,
             moe_w_gate=moe_w_gate, moe_w_up=moe_w_up, moe_w_down=moe_w_down)
    depth = w_in.shape[0]
    nb = c_prompt.shape[0]
    lps = [_layer_params(l, p) for l in range(depth)]
    bias = _swa_bias(rel_table)
    mod = _mod_call(jnp.concatenate([c_prompt, c_sample], axis=0), w_mod, b_mod)
    fg = final_norm_g[None, :]
    y_p, gla_p, k_p, v_p, _ = _run_group(x_prompt, mod[:, :nb], lps, bias, fg, True, None, None, None)
    y_s, gla_s, k_s, v_s, vc_s = _run_group(x_sample, mod[:, nb:], lps, bias, fg, False,
                                            state_gla, cache_swa_k, cache_swa_v)
    return (y_p, y_s, gla_p, k_p, v_p, gla_s, k_s, v_s, vc_s)
```

```python
import functools

import jax
import jax.numpy as jnp
from jax import lax
from jax.experimental import pallas as pl
from jax.experimental.pallas import tpu as pltpu

F32 = jnp.float32
BF16 = jnp.bfloat16
U32 = jnp.uint32
HIGHEST = lax.Precision.HIGHEST

D_MODEL = 1024
CHUNK = 64
EPS = 1e-6
GLA_HEADS = 4
GLA_DK = 128
GLA_DV = 256
GLA_RANK = 16
GLA_TAU = 16.0
SWA_HEADS = 16
SWA_KV_HEADS = 4
SWA_GROUP = 4
SWA_HD = 64
WINDOW = 128
REL_BUCKETS = 32
REL_MAX_DIST = 128
CMLP_CHUNK = 128
CMLP_GROUPS = 4
CMLP_WIDTH = 1024
CMLP_GW = 256
MOE_GROUPS = 4
MOE_PER_GROUP = 8
N_EXPERTS = 32
MOE_HIDDEN = 256

LANES = 128
SUBLANES = 8
VMEM_LIMIT = 62 * 1024 * 1024

TM_PROJ = 512
TB_GLA = 4096
TQ_SWA = 1024
TM_MOE = 2048
EPS_MOE = 4


def _moe_block_rows(tm):
    return max(16, -(-(tm * 5 // 64) // 16) * 16)
ROUTE_W = 128


def _cparams(sem):
    return pltpu.CompilerParams(dimension_semantics=sem, vmem_limit_bytes=VMEM_LIMIT)


def _dot(a, b):
    return jnp.dot(a, b, preferred_element_type=F32)


def _dot_nt(a, b):
    return lax.dot_general(a, b, (((1,), (1,)), ((), ())), preferred_element_type=F32)


def _dot_tn(a, b):
    return lax.dot_general(a, b, (((0,), (0,)), ((), ())), preferred_element_type=F32)


def _sigmoid(x):
    return 1.0 / (1.0 + jnp.exp(-x))


def _modulate(x, g, shift, scale):
    ms = jnp.mean(x * x, axis=-1, keepdims=True)
    return x * lax.rsqrt(ms + EPS) * g * (1.0 + scale) + shift


def _mod_kernel(c_ref, w_ref, b_ref, o_ref):
    c = c_ref[...]
    s = c * _sigmoid(c)
    o_ref[0] = jnp.dot(s, w_ref[0], precision=HIGHEST, preferred_element_type=F32) + b_ref[0]


def _mod_call(c_all, w_mod, b_mod):
    depth, d, n = w_mod.shape
    r = c_all.shape[0]
    bn = 2048
    return pl.pallas_call(
        _mod_kernel,
        grid=(depth, n // bn),
        in_specs=[
            pl.BlockSpec((r, d), lambda l, j: (0, 0)),
            pl.BlockSpec((1, d, bn), lambda l, j: (l, 0, j)),
            pl.BlockSpec((1, 1, bn), lambda l, j: (l, 0, j)),
        ],
        out_specs=pl.BlockSpec((1, r, bn), lambda l, j: (l, 0, j)),
        out_shape=jax.ShapeDtypeStruct((depth, r, n), F32),
        compiler_params=_cparams(("arbitrary", "arbitrary")),
        name="mod",
    )(c_all, w_mod, b_mod.reshape(depth, 1, n))


def _inproj_kernel(*refs, n, emit_v):
    if emit_v:
        (x_ref, sh_ref, sc_ref, g_ref, wa_ref, wq_ref, wkv_ref, wz_ref, wg_ref, lg_ref, lb_ref, cw_ref, cb_ref,
         pa_ref, qb_ref, kv_ref, oc_ref, ga_ref, v_ref) = refs
    else:
        (x_ref, sh_ref, sc_ref, g_ref, wa_ref, wq_ref, wkv_ref, wz_ref, wg_ref, lg_ref, lb_ref, cw_ref, cb_ref,
         pa_ref, qb_ref, kv_ref, oc_ref, ga_ref) = refs
    tm = x_ref.shape[0]
    h = _modulate(x_ref[...], g_ref[...], sh_ref[...], sc_ref[...]).astype(BF16)
    col = 512

    def proj(w_ref, o_ref):
        for j in range(0, w_ref.shape[1], col):
            o_ref[:, j:j + col] = _dot(h, w_ref[:, j:j + col]).astype(o_ref.dtype)

    zs = []
    for j in range(0, 2 * CMLP_WIDTH, col):
        z = _dot(h, wz_ref[:, j:j + col])
        zs.append(0.5 * z * (1.0 + jnp.tanh(0.7978845608028654 * (z + 0.044715 * (z * z * z)))))
    u = jnp.concatenate(zs[:len(zs) // 2], axis=1)
    v = jnp.concatenate(zs[len(zs) // 2:], axis=1)
    proj(wa_ref, pa_ref)
    xc = v - jnp.mean(v, axis=-1, keepdims=True)
    var = jnp.mean(xc * xc, axis=-1, keepdims=True)
    vn = xc * lax.rsqrt(var + EPS) * lg_ref[...] + lb_ref[...]
    if emit_v:
        v_ref[...] = vn
    vb = vn.astype(BF16)
    for ch in range(tm // n):
        rs = slice(ch * n, (ch + 1) * n)
        for g in range(CMLP_GROUPS):
            cs = slice(g * CMLP_GW, (g + 1) * CMLP_GW)
            mix = _dot(cw_ref[g], vb[rs, cs]) + cb_ref[:, cs]
            oc_ref[rs, cs] = (u[rs, cs] * mix).astype(oc_ref.dtype)
    proj(wq_ref, qb_ref)
    proj(wkv_ref, kv_ref)
    ga_ref[...] = _dot(h, wg_ref[...])


def _row_spec(tm, width, per_seq_tiles):
    if per_seq_tiles is None:
        return pl.BlockSpec((tm, width), lambda i: (i, 0))
    return pl.BlockSpec((None, 1, width), lambda i: (i // per_seq_tiles, 0, 0))


def _const_spec(shape):
    nd = len(shape)
    return pl.BlockSpec(shape, lambda i: (0,) * nd, pipeline_mode=pl.Buffered(1))


def _inproj_call(x, sh, sc, g, wa, wq, wkv, wz, wg, lg, lb, cw, cb, tm, per_seq_tiles, n, emit_v):
    ntok, d = x.shape
    outs = [
        jax.ShapeDtypeStruct((ntok, wa.shape[1]), BF16),
        jax.ShapeDtypeStruct((ntok, wq.shape[1]), BF16),
        jax.ShapeDtypeStruct((ntok, wkv.shape[1]), F32),
        jax.ShapeDtypeStruct((ntok, CMLP_WIDTH), BF16),
        jax.ShapeDtypeStruct((ntok, LANES), F32),
    ]
    if emit_v:
        outs.append(jax.ShapeDtypeStruct((ntok, CMLP_WIDTH), F32))
    res = pl.pallas_call(
        functools.partial(_inproj_kernel, n=n, emit_v=emit_v),
        grid=(ntok // tm,),
        in_specs=[
            pl.BlockSpec((tm, d), lambda i: (i, 0)),
            _row_spec(tm, d, per_seq_tiles),
            _row_spec(tm, d, per_seq_tiles),
            _const_spec((1, d)),
            _const_spec(wa.shape), _const_spec(wq.shape), _const_spec(wkv.shape),
            _const_spec(wz.shape), _const_spec(wg.shape),
            _const_spec((1, CMLP_WIDTH)), _const_spec((1, CMLP_WIDTH)),
            _const_spec(cw.shape), _const_spec(cb.shape),
        ],
        out_specs=[pl.BlockSpec((tm, o.shape[1]), lambda i: (i, 0)) for o in outs],
        out_shape=outs,
        compiler_params=_cparams(("arbitrary",)),
        name="inproj",
    )(x, sh, sc, g, wa, wq, wkv, wz, wg, lg, lb, cw, cb)
    return res if emit_v else (*res, None)


def _gla_kernel(*refs, nblk, has_state):
    if has_state:
        (q_ref, k_ref, v_ref, r_ref, ga_ref, wg_ref, bg_ref, ng_ref, s0_ref,
         o_ref, sfin_ref, st_ref) = refs
    else:
        (q_ref, k_ref, v_ref, r_ref, ga_ref, wg_ref, bg_ref, ng_ref,
         o_ref, sfin_ref, st_ref) = refs
        s0_ref = None
    t = pl.program_id(2)

    @pl.when(t == 0)
    def _():
        if has_state:
            st_ref[...] = s0_ref[0, 0].T
        else:
            st_ref[...] = jnp.zeros_like(st_ref)

    row = lax.broadcasted_iota(jnp.int32, (CHUNK, CHUNK), 0)
    colm = lax.broadcasted_iota(jnp.int32, (CHUNK, CHUNK), 1)
    causal = row >= colm
    tril = causal.astype(F32)
    gl = _dot(ga_ref[...].astype(BF16), wg_ref[...].astype(BF16)) + bg_ref[...]
    la = (jnp.minimum(gl, 0.0) - jnp.log(1.0 + jnp.exp(-jnp.abs(gl)))) / GLA_TAU
    sls = [slice(c * CHUNK, (c + 1) * CHUNK) for c in range(nblk)]
    cum = [jnp.dot(tril, la[sl], precision=HIGHEST, preferred_element_type=F32) for sl in sls]
    qts, kts, kds, decs = [], [], [], []
    for c, sl in enumerate(sls):
        b = cum[c]
        bl = b[CHUNK - 1:CHUNK, :]
        q = q_ref[sl, :].astype(F32) * (GLA_DK ** -0.5)
        k = k_ref[sl, :].astype(F32)
        qts.append((q * jnp.exp(b)).astype(BF16))
        kts.append((k * jnp.exp(-b)).astype(BF16))
        kds.append((k * jnp.exp(bl - b)).astype(BF16))
        decs.append(jnp.exp(bl))
    amats = [jnp.where(causal, _dot_nt(qts[c], kts[c]), 0.0).astype(BF16) for c in range(nblk)]
    kvs = [_dot_tn(v_ref[sls[c], :], kds[c]) for c in range(nblk)]
    st = st_ref[...]
    states = []
    for c in range(nblk):
        states.append(st.astype(BF16))
        st = st * decs[c] + kvs[c]
    st_ref[...] = st
    for c, sl in enumerate(sls):
        o = _dot(amats[c], v_ref[sl, :]) + _dot_nt(qts[c], states[c])
        on = o * lax.rsqrt(jnp.mean(o * o, axis=-1, keepdims=True) + EPS) * ng_ref[...]
        r = r_ref[sl, :].astype(F32)
        o_ref[sl, :] = (on * (r * _sigmoid(r))).astype(o_ref.dtype)

    @pl.when(t == pl.num_programs(2) - 1)
    def _():
        sfin_ref[0, 0] = st_ref[...].T


def _gla_call(pa, ga, wg, bg, ng, s0, nseq, seqlen, tb):
    n = pa.shape[0]
    nt = seqlen // tb
    kq = GLA_HEADS
    vq = (2 * GLA_HEADS * GLA_DK) // GLA_DV
    rq = vq + GLA_HEADS
    in_specs = [
        pl.BlockSpec((tb, GLA_DK), lambda b, h, t: (b * nt + t, h)),
        pl.BlockSpec((tb, GLA_DK), lambda b, h, t: (b * nt + t, kq + h)),
        pl.BlockSpec((tb, GLA_DV), lambda b, h, t: (b * nt + t, vq + h)),
        pl.BlockSpec((tb, GLA_DV), lambda b, h, t: (b * nt + t, rq + h)),
        pl.BlockSpec((tb, LANES), lambda b, h, t: (b * nt + t, 0)),
        pl.BlockSpec((LANES, GLA_DK), lambda b, h, t: (0, h)),
        pl.BlockSpec((1, GLA_DK), lambda b, h, t: (0, h)),
        pl.BlockSpec((1, GLA_DV), lambda b, h, t: (0, 0)),
    ]
    args = [pa, pa, pa, pa, ga, wg, bg, ng]
    if s0 is not None:
        in_specs.append(pl.BlockSpec((1, 1, GLA_DK, GLA_DV), lambda b, h, t: (b, h, 0, 0)))
        args.append(s0)
    return pl.pallas_call(
        functools.partial(_gla_kernel, nblk=tb // CHUNK, has_state=s0 is not None),
        grid=(nseq, GLA_HEADS, nt),
        in_specs=in_specs,
        out_specs=[
            pl.BlockSpec((tb, GLA_DV), lambda b, h, t: (b * nt + t, h)),
            pl.BlockSpec((1, 1, GLA_DK, GLA_DV), lambda b, h, t: (b, h, 0, 0)),
        ],
        out_shape=(
            jax.ShapeDtypeStruct((n, GLA_HEADS * GLA_DV), BF16),
            jax.ShapeDtypeStruct((nseq, GLA_HEADS, GLA_DK, GLA_DV), F32),
        ),
        scratch_shapes=[pltpu.VMEM((GLA_DV, GLA_DK), F32)],
        compiler_params=_cparams(("arbitrary", "arbitrary", "arbitrary")),
        name="gla",
    )(*args)


def _swa_kernel(q_ref, kc_ref, vc_ref, kp_ref, vp_ref, bias_ref, sink_ref, o_ref, *, tq, mask_start):
    blk = pl.program_id(1)
    kall = jnp.concatenate([kp_ref[...], kc_ref[...]], axis=0).astype(BF16)
    vall = jnp.concatenate([vp_ref[...], vc_ref[...]], axis=0).astype(BF16)
    lwin = WINDOW + CHUNK
    cols = SWA_GROUP * CHUNK
    for c in range(tq // CHUNK):
        need_mask = mask_start and c * CHUNK < WINDOW
        if need_mask:
            first_pos = blk * tq + c * CHUNK - WINDOW
            valid = lax.broadcasted_iota(jnp.int32, (lwin, cols), 0) + first_pos >= 0
        scores = []
        for x in range(SWA_KV_HEADS):
            kw = kall[c * CHUNK:c * CHUNK + lwin, x * SWA_HD:(x + 1) * SWA_HD]
            base = x * SWA_GROUP * SWA_HD
            qs = jnp.concatenate(
                [q_ref[c * CHUNK:(c + 1) * CHUNK, base + g * SWA_HD:base + (g + 1) * SWA_HD]
                 for g in range(SWA_GROUP)], axis=0)
            scores.append(_dot_nt(kw, qs))
        probs = []
        for x in range(SWA_KV_HEADS):
            s = scores[x] + bias_ref[x]
            if need_mask:
                s = jnp.where(valid, s, -1e30)
            sink = sink_ref[x]
            m = jnp.maximum(jnp.max(s, axis=0, keepdims=True), sink)
            p = jnp.exp(s - m)
            den = jnp.sum(p, axis=0, keepdims=True) + jnp.exp(sink - m)
            probs.append((p.astype(BF16), 1.0 / den))
        for x in range(SWA_KV_HEADS):
            vw = vall[c * CHUNK:c * CHUNK + lwin, x * SWA_HD:(x + 1) * SWA_HD]
            base = x * SWA_GROUP * SWA_HD
            vpad = jnp.concatenate([vw, jnp.zeros_like(vw)], axis=1)
            p, rden = probs[x]
            o = (_dot_tn(vpad, p) * rden).T
            o_ref[c * CHUNK:(c + 1) * CHUNK, base:base + SWA_GROUP * SWA_HD] = jnp.concatenate(
                [o[g * CHUNK:(g + 1) * CHUNK, :SWA_HD] for g in range(SWA_GROUP)], axis=1).astype(o_ref.dtype)


def _swa_call(qb, k_cur, v_cur, k_prev, v_prev, bias, sink, nseq, seqlen, tq, prompt):
    n = qb.shape[0]
    nt = seqlen // tq
    kvw = SWA_KV_HEADS * SWA_HD
    per_prev = tq // WINDOW if prompt else None

    def cur_spec(cb):
        return pl.BlockSpec((tq, kvw), lambda b, t: (b * nt + t, cb))

    def prev_spec(cb):
        if prompt:
            nprev = seqlen // WINDOW
            return pl.BlockSpec((WINDOW, kvw),
                                lambda b, t: (b * nprev + jnp.maximum(t * per_prev - 1, 0), cb))
        return pl.BlockSpec((None, WINDOW, kvw), lambda b, t: (b, 0, cb))

    return pl.pallas_call(
        functools.partial(_swa_kernel, tq=tq, mask_start=prompt),
        grid=(nseq, nt),
        in_specs=[
            pl.BlockSpec((tq, SWA_HEADS * SWA_HD), lambda b, t: (b * nt + t, 0)),
            cur_spec(k_cur[1]), cur_spec(v_cur[1]), prev_spec(k_prev[1]), prev_spec(v_prev[1]),
            pl.BlockSpec(bias.shape, lambda b, t: (0, 0, 0)),
            pl.BlockSpec(sink.shape, lambda b, t: (0, 0, 0)),
        ],
        out_specs=pl.BlockSpec((tq, SWA_HEADS * SWA_HD), lambda b, t: (b * nt + t, 0)),
        out_shape=jax.ShapeDtypeStruct((n, SWA_HEADS * SWA_HD), BF16),
        compiler_params=_cparams(("arbitrary", "arbitrary")),
        name="swa",
    )(qb, k_cur[0], v_cur[0], k_prev[0], v_prev[0], bias, sink)


def _merge_kernel(x_ref, sh1_ref, sc1_ref, gt1_ref, sh2_ref, sc2_ref, g1_ref, g2_ref,
                  oa_ref, ob_ref, oc_ref, wmg_ref, bmg_ref, wa_ref, wb_ref, wc_ref, wo_ref,
                  wrh_ref, wrw_ref, br_ref, xo_ref, h2_ref, lg_ref):
    x = x_ref[...]
    h = _modulate(x, g1_ref[...], sh1_ref[...], sc1_ref[...]).astype(BF16)
    d = x.shape[1]
    m = None
    for i, (o_ref, w_ref) in enumerate(((oa_ref, wa_ref), (ob_ref, wb_ref), (oc_ref, wc_ref))):
        gate = _sigmoid(_dot(h, wmg_ref[:, i * d:(i + 1) * d]) + bmg_ref[:, i * d:(i + 1) * d])
        term = gate * _dot(o_ref[...], w_ref[...])
        m = term if m is None else m + term
    xn = x + gt1_ref[...] * _dot(m.astype(BF16), wo_ref[...])
    xo_ref[...] = xn
    h2 = _modulate(xn, g2_ref[...], sh2_ref[...], sc2_ref[...])
    hi = h2.astype(BF16)
    hf = hi.astype(F32)
    lo = (h2 - hf).astype(BF16)
    h2_ref[...] = ((lax.bitcast_convert_type(hf[:, :d // 2], U32) >> 16)
                   | (lax.bitcast_convert_type(hf[:, d // 2:], U32) & jnp.uint32(0xFFFF0000)))
    both = _dot(hi, wrw_ref[...])
    lg_ref[...] = (both[:, :ROUTE_W] + both[:, ROUTE_W:] + _dot(lo, wrh_ref[...])) + br_ref[...]


def _merge_call(x, mods, g1, g2, oa, ob, oc, wmg, bmg, wa, wb, wc, wo, wrh, wrw, br, tm, per_seq_tiles):
    n, d = x.shape
    tok = lambda w: pl.BlockSpec((tm, w), lambda i: (i, 0))
    in_specs = [tok(d)] + [_row_spec(tm, d, per_seq_tiles) for _ in mods]
    in_specs += [_const_spec((1, d)), _const_spec((1, d)), tok(d), tok(d), tok(d)]
    in_specs += [_const_spec(a.shape) for a in (wmg, bmg, wa, wb, wc, wo, wrh, wrw, br)]
    return pl.pallas_call(
        _merge_kernel,
        grid=(n // tm,),
        in_specs=in_specs,
        out_specs=[tok(d), tok(d // 2), tok(ROUTE_W)],
        out_shape=(
            jax.ShapeDtypeStruct((n, d), F32),
            jax.ShapeDtypeStruct((n, d // 2), U32),
            jax.ShapeDtypeStruct((n, ROUTE_W), F32),
        ),
        compiler_params=_cparams(("arbitrary",)),
        name="merge",
    )(x, *mods, g1, g2, oa, ob, oc, wmg, bmg, wa, wb, wc, wo, wrh, wrw, br)


def _route_kernel(lg_ref, pos_ref, w_ref, meta_ref):
    lg = lg_ref[...]
    tm = lg.shape[0]
    lane = lax.broadcasted_iota(jnp.int32, lg.shape, 1)
    big = jnp.int32(1 << 20)
    ninf = -jnp.inf
    is_g = (lane >= N_EXPERTS) & (lane < N_EXPERTS + MOE_GROUPS)
    gl = jnp.where(is_g, lg, ninf)
    gmax = jnp.max(gl, axis=-1, keepdims=True)
    g_idx = jnp.min(jnp.where(gl == gmax, lane - N_EXPERTS, big), axis=-1, keepdims=True)
    g_val = 1.0 / jnp.sum(jnp.exp(gl - gmax), axis=-1, keepdims=True)
    in_grp = (lane < N_EXPERTS) & ((lane // MOE_PER_GROUP) == g_idx)
    el = jnp.where(in_grp, lg, ninf)
    v0 = jnp.max(el, axis=-1, keepdims=True)
    i0 = jnp.min(jnp.where(el == v0, lane, big), axis=-1, keepdims=True)
    el1 = jnp.where(lane == i0, ninf, el)
    v1 = jnp.max(el1, axis=-1, keepdims=True)
    i1 = jnp.min(jnp.where(el1 == v1, lane, big), axis=-1, keepdims=True)
    e10 = jnp.exp(v1 - v0)
    w0 = g_val / (1.0 + e10)
    w1 = g_val * e10 / (1.0 + e10)

    hit0 = lane == i0
    hit1 = lane == i1
    assign = (hit0 | hit1)
    ab = assign.astype(BF16)
    sb = min(tm, 256)
    r_i = lax.broadcasted_iota(jnp.int32, (sb, sb), 0)
    c_i = lax.broadcasted_iota(jnp.int32, (sb, sb), 1)
    before = (c_i < r_i).astype(BF16)
    ranks = []
    cnt = jnp.zeros((1, lg.shape[1]), F32)
    for q in range(tm // sb):
        a_q = ab[q * sb:(q + 1) * sb, :]
        ranks.append(_dot(before, a_q) + cnt)
        cnt = cnt + jnp.sum(a_q.astype(F32), axis=0, keepdims=True)
    rank = jnp.concatenate(ranks, axis=0)
    cnt8 = jnp.broadcast_to(cnt, (SUBLANES, lg.shape[1]))
    inc = cnt8
    lane8 = lax.broadcasted_iota(jnp.int32, inc.shape, 1)
    sh = 1
    while sh < N_EXPERTS:
        inc = inc + jnp.where(lane8 >= sh, pltpu.roll(inc, sh, 1), 0.0)
        sh *= 2
    off8 = inc - cnt8
    slot = rank + off8[0:1, :]
    p0 = jnp.sum(jnp.where(hit0, slot, 0.0), axis=-1, keepdims=True)
    p1 = jnp.sum(jnp.where(hit1, slot, 0.0), axis=-1, keepdims=True)
    pos_t = jnp.where(lane == 0, p0, jnp.where(lane == 1, p1, 0.0)).T
    pos_ref[...] = pos_t[:SUBLANES, :].astype(jnp.int32)
    w_ref[...] = jnp.where(lane == 0, w0, jnp.where(lane == 1, w1, 0.0))
    row8 = lax.broadcasted_iota(jnp.int32, inc.shape, 0)
    meta_ref[0] = jnp.where(row8 == 0, off8, cnt8).astype(jnp.int32)


def _route_call(lg, tm):
    n = lg.shape[0]
    nt = n // tm
    return pl.pallas_call(
        _route_kernel,
        grid=(nt,),
        in_specs=[pl.BlockSpec((tm, ROUTE_W), lambda i: (i, 0))],
        out_specs=[
            pl.BlockSpec((SUBLANES, tm), lambda i: (i, 0)),
            pl.BlockSpec((tm, ROUTE_W), lambda i: (i, 0)),
            pl.BlockSpec((1, SUBLANES, ROUTE_W), lambda i: (i, 0, 0)),
        ],
        out_shape=(
            jax.ShapeDtypeStruct((nt * SUBLANES, tm), jnp.int32),
            jax.ShapeDtypeStruct((n, ROUTE_W), F32),
            jax.ShapeDtypeStruct((nt, SUBLANES, ROUTE_W), jnp.int32),
        ),
        compiler_params=_cparams(("arbitrary",)),
        name="route",
    )(lg)


def _moe_kernel(pos_ref, meta_ref, h_ref, x_ref, w2_ref, gt_ref, fg_ref, wg_ref, wu_ref, wd_ref,
                o_ref, xt_ref, ct0_ref, ct1_ref, slot0_ref, slot1_ref, *, tm, c, eps, nt, final):
    g = pl.program_id(0)
    e = pl.program_id(1)
    nq = D_MODEL // 2 // LANES
    half = D_MODEL // 2
    nsteps = N_EXPERTS // eps
    half_steps = nsteps // 2
    rows_step = tm // half_steps
    per_exp = rows_step // eps
    rc = min(per_exp, 256)
    nslot = 2 * tm
    blocks_on = jnp.logical_and(g >= 1, g <= nt).astype(jnp.int32)
    mbase = jnp.clip(g - 1, 0, nt - 1) * (2 * N_EXPERTS)
    pcomb = jnp.clip(g - 2, 0, nt - 1) * nslot
    pdisp = jnp.minimum(g, nt - 1) * nslot

    def read_rows(ref, r0, nrows):
        return [ref[pl.ds(r0 + s, nrows, stride=nq), :] for s in range(nq)]

    def words_to_f32(ws):
        los = [lax.bitcast_convert_type(w << 16, F32) for w in ws]
        his = [lax.bitcast_convert_type(w & jnp.uint32(0xFFFF0000), F32) for w in ws]
        return jnp.concatenate(los + his, axis=1)

    @pl.when(jnp.logical_and(g == 0, e == 0))
    def _():
        slot0_ref[...] = jnp.zeros(slot0_ref.shape, U32)
        slot1_ref[...] = jnp.zeros(slot1_ref.shape, U32)

    def stage(cur_ref, oth_ref):
        def block(b, j, start, cnt):
            r0 = pl.multiple_of((start + b * c) * nq, nq)
            ws = read_rows(cur_ref, r0, c)
            xb = words_to_f32(ws).astype(BF16)
            gate = _dot(xb, wg_ref[j])
            hid = (gate * _sigmoid(gate)) * _dot(xb, wu_ref[j])
            yb = _dot(hid.astype(BF16), wd_ref[j]).astype(BF16).astype(F32)
            yw = ((lax.bitcast_convert_type(yb[:, :half], U32) >> 16)
                  | (lax.bitcast_convert_type(yb[:, half:], U32) & jnp.uint32(0xFFFF0000)))
            keep = lax.broadcasted_iota(jnp.int32, (c, 1), 0) < (cnt - b * c)
            for s in range(nq):
                cur_ref[pl.ds(r0 + s, c, stride=nq), :] = jnp.where(
                    keep, yw[:, s * LANES:(s + 1) * LANES], ws[s])

        def expert(j, copies):
            ex = e * eps + j
            start = meta_ref[mbase + ex]
            cnt = meta_ref[mbase + N_EXPERTS + ex] * blocks_on
            block(0, j, start, cnt)
            copies(j)

            def more(b, carry):
                block(b, j, start, cnt)
                return carry
            lax.fori_loop(1, (cnt + c - 1) // c, more, 0)

        @pl.when(e < half_steps)
        def _():
            tok0 = e * rows_step

            def gather(j):
                lo = j * per_exp
                for tl in range(lo, lo + per_exp):
                    for k, ct_ref in ((0, ct0_ref), (1, ct1_ref)):
                        p = pos_ref[pcomb + k * tm + tok0 + tl]
                        ct_ref[pl.ds(tl * nq, nq), :] = oth_ref[pl.ds(pl.multiple_of(p * nq, nq), nq), :]
                for r in range(lo, lo + per_exp, rc):
                    rows = slice(r, r + rc)
                    wts = w2_ref[rows, :]
                    acc = (words_to_f32(read_rows(ct0_ref, r * nq, rc)) * wts[:, 0:1]
                           + words_to_f32(read_rows(ct1_ref, r * nq, rc)) * wts[:, 1:2])
                    gate = gt_ref[...] if gt_ref.shape[0] == 1 else gt_ref[rows, :]
                    xn = x_ref[rows, :] + gate * acc
                    if final:
                        xn = xn * lax.rsqrt(jnp.mean(xn * xn, axis=-1, keepdims=True) + EPS) * fg_ref[...]
                    o_ref[rows, :] = xn
            for j in range(eps):
                expert(j, gather)

        @pl.when(e >= half_steps)
        def _():
            @pl.when(e == half_steps)
            def _():
                for s in range(nq):
                    xt_ref[pl.ds(s, tm, stride=nq), :] = h_ref[:, s * LANES:(s + 1) * LANES]
                oth_ref[pl.ds(nslot * nq, c * nq), :] = jnp.zeros((c * nq, LANES), U32)
            tok0 = (e - half_steps) * rows_step

            def scatter(j):
                for tl in range(j * per_exp, (j + 1) * per_exp):
                    t = tok0 + tl
                    row = xt_ref[pl.ds(pl.multiple_of(t * nq, nq), nq), :]
                    for k in range(2):
                        p = pos_ref[pdisp + k * tm + t]
                        oth_ref[pl.ds(pl.multiple_of(p * nq, nq), nq), :] = row
            for j in range(eps):
                expert(j, scatter)

    @pl.when(g % 2 == 1)
    def _():
        stage(slot0_ref, slot1_ref)

    @pl.when(g % 2 == 0)
    def _():
        stage(slot1_ref, slot0_ref)


def _moe_call(pos, meta, h2, x, w2, gt, fg, wg, wu, wd, tm, c, eps, per_seq_tiles, final):
    n, d = x.shape
    nt = n // tm
    nq = d // 2 // LANES
    nsteps = N_EXPERTS // eps
    rows_step = tm // (nsteps // 2)

    half_steps = nsteps // 2

    def done(g, e):
        return jnp.where(g >= 2, jnp.minimum(g - 2, nt - 1) * half_steps + jnp.minimum(e, half_steps - 1), 0)

    def incoming(g, e):
        return jnp.clip(jnp.where(e >= half_steps - 1, g, g - 1), 0, nt - 1)

    if per_seq_tiles is None:
        gt_spec = pl.BlockSpec((rows_step, d), lambda g, e, *_: (done(g, e), 0))
    else:
        gt_spec = pl.BlockSpec((None, 1, d),
                               lambda g, e, *_: (jnp.clip(g - 2, 0, nt - 1) // per_seq_tiles, 0, 0))
    grid_spec = pltpu.PrefetchScalarGridSpec(
        num_scalar_prefetch=2,
        grid=(nt + 2, nsteps),
        in_specs=[
            pl.BlockSpec((tm, d // 2), lambda g, e, *_: (incoming(g, e), 0)),
            pl.BlockSpec((rows_step, d), lambda g, e, *_: (done(g, e), 0)),
            pl.BlockSpec((rows_step, ROUTE_W), lambda g, e, *_: (done(g, e), 0)),
            gt_spec,
            pl.BlockSpec((1, d), lambda g, e, *_: (0, 0)),
            pl.BlockSpec((eps, d, MOE_HIDDEN), lambda g, e, *_: (e, 0, 0)),
            pl.BlockSpec((eps, d, MOE_HIDDEN), lambda g, e, *_: (e, 0, 0)),
            pl.BlockSpec((eps, MOE_HIDDEN, d), lambda g, e, *_: (e, 0, 0)),
        ],
        out_specs=pl.BlockSpec((rows_step, d), lambda g, e, *_: (done(g, e), 0)),
        scratch_shapes=[
            pltpu.VMEM((tm * nq, LANES), U32),
            pltpu.VMEM((rows_step * nq, LANES), U32),
            pltpu.VMEM((rows_step * nq, LANES), U32),
            pltpu.VMEM(((2 * tm + c) * nq, LANES), U32),
            pltpu.VMEM(((2 * tm + c) * nq, LANES), U32),
        ],
    )
    return pl.pallas_call(
        functools.partial(_moe_kernel, tm=tm, c=c, eps=eps, nt=nt, final=final),
        grid_spec=grid_spec,
        out_shape=jax.ShapeDtypeStruct((n, d), F32),
        compiler_params=_cparams(("arbitrary", "arbitrary")),
        name="moe",
    )(pos, meta, h2, x, w2, gt, fg, wg, wu, wd)


def _rel_bucket(rel):
    nb = REL_BUCKETS // 2
    max_exact = nb // 2
    ret = jnp.where(rel > 0, nb, 0)
    nabs = jnp.abs(rel)
    nf = jnp.maximum(nabs, 1).astype(F32)
    large = max_exact + (jnp.log(nf / max_exact) / jnp.log(REL_MAX_DIST / max_exact)
                         * (nb - max_exact)).astype(jnp.int32)
    large = jnp.minimum(large, nb - 1)
    return ret + jnp.where(nabs < max_exact, nabs, large)


def _swa_bias(table):
    lwin = WINDOW + CHUNK
    rel = jnp.arange(lwin)[None, :] - WINDOW - jnp.arange(CHUNK)[:, None]
    onehot = (_rel_bucket(rel)[..., None] == jnp.arange(REL_BUCKETS)).astype(F32)
    b = jnp.einsum('qlb,bh->hql', onehot, table.astype(F32), precision=HIGHEST)
    b = b.reshape(SWA_KV_HEADS, SWA_GROUP * CHUNK, lwin)
    return jnp.transpose(b, (0, 2, 1))


def _layer_params(l, p):
    d = D_MODEL
    w_in = p['w_in'][l]
    o = 0
    cols = {}
    for name, width in (('qa', 512), ('ka', 512), ('va', 1024), ('ga', GLA_RANK), ('ra', 1024),
                        ('qb', 1024), ('kb', 256), ('vb', 256), ('zc', 2048)):
        cols[name] = w_in[:, o:o + width]
        o += width
    lp = {}
    lp['wa'] = jnp.concatenate([cols['qa'], cols['ka'], cols['va'], cols['ra']], axis=1).astype(BF16)
    lp['wq'] = (cols['qb'] * (SWA_HD ** -0.5)).astype(BF16)
    lp['wkv'] = jnp.concatenate([cols['kb'], cols['vb']], axis=1).astype(BF16)
    lp['wz'] = cols['zc'].astype(BF16)
    lp['wg'] = jnp.pad(cols['ga'], ((0, 0), (0, LANES - GLA_RANK))).astype(BF16)
    lp['gla_wg'] = jnp.pad(p['gla_w_gate'][l], ((0, LANES - GLA_RANK), (0, 0)))
    lp['gla_bg'] = p['gla_b_gate'][l][None, :]
    lp['gla_ng'] = p['gla_norm_g'][l][None, :]
    sink = p['swa_sink'][l].astype(F32).reshape(SWA_KV_HEADS, SWA_GROUP, 1)
    lp['sink'] = jnp.broadcast_to(sink[:, :, None, :], (SWA_KV_HEADS, SWA_GROUP, CHUNK, 1)).reshape(
        SWA_KV_HEADS, 1, SWA_GROUP * CHUNK)
    lp['ln_g'] = p['cmlp_ln_g'][l][None, :]
    lp['ln_b'] = p['cmlp_ln_b'][l][None, :]
    ws = p['cmlp_ws'][l]
    pos = jnp.arange(CMLP_CHUNK)
    mask = (pos[None, :] // CHUNK) <= (pos[:, None] // CHUNK)
    lp['cw_full'] = jnp.where(mask[None], ws, 0.0).astype(BF16)
    lp['cw_half'] = ws[:, :CHUNK, :CHUNK].astype(BF16)
    bs = p['cmlp_bs'][l]
    bs_full = jnp.repeat(bs.T, CMLP_GW, axis=1).astype(F32)
    lp['cb_full'] = bs_full
    lp['cb_half'] = bs_full[:CHUNK]
    lp['wmg'] = p['w_merge_gate'][l].astype(BF16)
    lp['bmg'] = p['b_merge_gate'][l][None, :]
    lp['wba'] = p['w_branch_a'][l].astype(BF16)
    lp['wbb'] = p['w_branch_b'][l].astype(BF16)
    lp['wbc'] = p['w_branch_c'][l].astype(BF16)
    lp['wo'] = p['w_out'][l].astype(BF16)
    wr = jnp.concatenate([p['moe_w_re'][l], p['moe_w_rg'][l]], axis=1)
    wr = jnp.pad(wr, ((0, 0), (0, ROUTE_W - wr.shape[1])))
    wrh = wr.astype(BF16)
    lp['wrh'] = wrh
    lp['wrw'] = jnp.concatenate([wrh, (wr - wrh.astype(F32)).astype(BF16)], axis=1)
    br = jnp.concatenate([p['moe_b_re'][l], p['moe_b_rg'][l]])
    lp['br'] = jnp.pad(br, (0, ROUTE_W - br.shape[0]))[None, :].astype(F32)
    lp['ewg'] = p['moe_w_gate'][l].astype(BF16)
    lp['ewu'] = p['moe_w_up'][l].astype(BF16)
    lp['ewd'] = p['moe_w_down'][l].astype(BF16)
    lp['n1'] = p['norm1_g'][l][None, :]
    lp['n2'] = p['norm2_g'][l][None, :]
    return lp


def _pick_tile(n, pref):
    t = min(n, pref)
    while n % t:
        t //= 2
    return t


def _run_group(x3, mod, lps, bias, final_g, prompt, state_gla, cache_k, cache_v):
    nseq, seqlen, d = x3.shape
    n = nseq * seqlen
    x = x3.reshape(n, d)
    depth = len(lps)
    tm = _pick_tile(seqlen, TM_PROJ) if prompt else _pick_tile(n, TM_PROJ // 2)
    tm_moe = _pick_tile(seqlen if prompt else n, TM_MOE)
    kvw = SWA_KV_HEADS * SWA_HD
    gla_out, k_out, v_out, vc_out = [], [], [], []
    for l, lp in enumerate(lps):
        m6 = mod[l].reshape(nseq, 6, d)
        if prompt:
            mods = [m6[:, j][:, None, :] for j in range(6)]
            pst = seqlen // tm
            pst_moe = seqlen // tm_moe
        else:
            mods = [jnp.broadcast_to(m6[:, j][:, None, :], (nseq, seqlen, d)).reshape(n, d) for j in range(6)]
            pst = None
            pst_moe = None
        sh1, sc1, gt1, sh2, sc2, gt2 = mods
        if prompt:
            cw, cb, nch = lp['cw_full'], lp['cb_full'], CMLP_CHUNK
        else:
            cw, cb, nch = lp['cw_half'], lp['cb_half'], seqlen
        pa, qb, kv, oc, ga, vc = _inproj_call(x, sh1, sc1, lp['n1'], lp['wa'], lp['wq'], lp['wkv'], lp['wz'],
                                              lp['wg'], lp['ln_g'], lp['ln_b'], cw, cb, tm, pst, nch, not prompt)
        if not prompt:
            vc_out.append(vc.reshape(nseq, seqlen, CMLP_WIDTH))
        tb = _pick_tile(seqlen, TB_GLA)
        s0 = None if prompt else state_gla[l]
        oa, s_new = _gla_call(pa, ga, lp['gla_wg'], lp['gla_bg'], lp['gla_ng'], s0, nseq, seqlen, tb)
        if prompt:
            tq = _pick_tile(seqlen, TQ_SWA)
            ob = _swa_call(qb, (kv, 0), (kv, 1), (kv, 0), (kv, 1), bias, lp['sink'], nseq, seqlen, tq, True)
            kv3 = kv.reshape(nseq, seqlen, 2, SWA_KV_HEADS, SWA_HD)
            k_state = kv3[:, -WINDOW:, 0]
            v_state = kv3[:, -WINDOW:, 1]
        else:
            lc = cache_k.shape[2]
            ck = cache_k[l].reshape(nseq, lc, kvw)
            cv = cache_v[l].reshape(nseq, lc, kvw)
            ob = _swa_call(qb, (kv, 0), (kv, 1), (ck, 0), (cv, 0), bias, lp['sink'], nseq, seqlen, seqlen, False)
            kv3 = kv.reshape(nseq, seqlen, 2, SWA_KV_HEADS, SWA_HD)
            k_state = jnp.concatenate([cache_k[l], kv3[:, :, 0]], axis=1)[:, -lc:]
            v_state = jnp.concatenate([cache_v[l], kv3[:, :, 1]], axis=1)[:, -lc:]
        x1, h2, lg = _merge_call(x, [sh1, sc1, gt1, sh2, sc2], lp['n1'], lp['n2'], oa, ob, oc,
                                 lp['wmg'], lp['bmg'], lp['wba'], lp['wbb'], lp['wbc'], lp['wo'],
                                 lp['wrh'], lp['wrw'], lp['br'], tm, pst)
        pos, w2, meta = _route_call(lg, tm_moe)
        meta_flat = jnp.concatenate([meta[:, 0, :N_EXPERTS], meta[:, 1, :N_EXPERTS]], axis=1).reshape(-1)
        pos_flat = pos.reshape(-1, SUBLANES, tm_moe)[:, :2, :].reshape(-1)
        x = _moe_call(pos_flat, meta_flat, h2, x1, w2, gt2, final_g, lp['ewg'], lp['ewu'], lp['ewd'],
                      tm_moe, _moe_block_rows(tm_moe), EPS_MOE, pst_moe, l == depth - 1)
        gla_out.append(s_new)
        k_out.append(k_state)
        v_out.append(v_state)
    y = x.reshape(nseq, seqlen, d)
    vc_stack = jnp.stack(vc_out) if vc_out else None
    return y, jnp.stack(gla_out), jnp.stack(k_out), jnp.stack(v_out), vc_stack


def kernel(x_prompt, x_sample, state_gla, cache_swa_k, cache_swa_v, c_prompt, c_sample, rel_table, final_norm_g, norm1_g, norm2_g, w_mod, b_mod, w_in, gla_w_gate, gla_b_gate, gla_norm_g, swa_sink, cmlp_ln_g, cmlp_ln_b, cmlp_ws, cmlp_bs, w_branch_a, w_branch_b, w_branch_c, w_merge_gate, b_merge_gate, w_out, moe_w_rg, moe_b_rg, moe_w_re, moe_b_re, moe_w_gate, moe_w_up, moe_w_down):
    p = dict(norm1_g=norm1_g, norm2_g=norm2_g, w_in=w_in, gla_w_gate=gla_w_gate, gla_b_gate=gla_b_gate,
             gla_norm_g=gla_norm_g, swa_sink=swa_sink, cmlp_ln_g=cmlp_ln_g, cmlp_ln_b=cmlp_ln_b,
             cmlp_ws=cmlp_ws, cmlp_bs=cmlp_bs, w_branch_a=w_branch_a, w_branch_b=w_branch_b,
             w_branch_c=w_branch_c, w_merge_gate=w_merge_gate, b_merge_gate=b_merge_gate, w_out=w_out,
             moe_w_rg=moe_w_rg, moe_b_rg=moe_b_rg, moe_w_re=moe_w_re, moe_b_re=moe_b_re,
             moe_w_gate=moe_w_gate, moe_w_up=moe_w_up, moe_w_down=moe_w_down)
    depth = w_in.shape[0]
    nb = c_prompt.shape[0]
    lps = [_layer_params(l, p) for l in range(depth)]
    bias = _swa_bias(rel_table)
    mod = _mod_call(jnp.concatenate([c_prompt, c_sample], axis=0), w_mod, b_mod)
    fg = final_norm_g[None, :]
    y_p, gla_p, k_p, v_p, _ = _run_group(x_prompt, mod[:, :nb], lps, bias, fg, True, None, None, None)
    y_s, gla_s, k_s, v_s, vc_s = _run_group(x_sample, mod[:, nb:], lps, bias, fg, False,
                                            state_gla, cache_swa_k, cache_swa_v)
    return (y_p, y_s, gla_p, k_p, v_p, gla_s, k_s, v_s, vc_s)
```

```python
import functools

import jax
import jax.numpy as jnp
from jax import lax
from jax.experimental import pallas as pl
from jax.experimental.pallas import tpu as pltpu

F32 = jnp.float32
BF16 = jnp.bfloat16
U32 = jnp.uint32
HIGHEST = lax.Precision.HIGHEST

D_MODEL = 1024
CHUNK = 64
EPS = 1e-6
GLA_HEADS = 4
GLA_DK = 128
GLA_DV = 256
GLA_RANK = 16
GLA_TAU = 16.0
SWA_HEADS = 16
SWA_KV_HEADS = 4
SWA_GROUP = 4
SWA_HD = 64
WINDOW = 128
REL_BUCKETS = 32
REL_MAX_DIST = 128
CMLP_CHUNK = 128
CMLP_GROUPS = 4
CMLP_WIDTH = 1024
CMLP_GW = 256
MOE_GROUPS = 4
MOE_PER_GROUP = 8
N_EXPERTS = 32
MOE_HIDDEN = 256

LANES = 128
SUBLANES = 8
VMEM_LIMIT = 62 * 1024 * 1024

TM_PROJ = 512
TB_GLA = 4096
TQ_SWA = 1024
TM_MOE = 2048
EPS_MOE = 4


def _moe_block_rows(tm):
    return max(16, -(-(tm * 5 // 64) // 16) * 16)
ROUTE_W = 128


def _cparams(sem):
    return pltpu.CompilerParams(dimension_semantics=sem, vmem_limit_bytes=VMEM_LIMIT)


def _dot(a, b):
    return jnp.dot(a, b, preferred_element_type=F32)


def _dot_nt(a, b):
    return lax.dot_general(a, b, (((1,), (1,)), ((), ())), preferred_element_type=F32)


def _dot_tn(a, b):
    return lax.dot_general(a, b, (((0,), (0,)), ((), ())), preferred_element_type=F32)


def _sigmoid(x):
    return 1.0 / (1.0 + jnp.exp(-x))


def _modulate(x, g, shift, scale):
    ms = jnp.mean(x * x, axis=-1, keepdims=True)
    return x * lax.rsqrt(ms + EPS) * g * (1.0 + scale) + shift


def _mod_kernel(c_ref, w_ref, b_ref, o_ref):
    c = c_ref[...]
    s = c * _sigmoid(c)
    o_ref[0] = jnp.dot(s, w_ref[0], precision=HIGHEST, preferred_element_type=F32) + b_ref[0]


def _mod_call(c_all, w_mod, b_mod):
    depth, d, n = w_mod.shape
    r = c_all.shape[0]
    bn = 2048
    return pl.pallas_call(
        _mod_kernel,
        grid=(depth, n // bn),
        in_specs=[
            pl.BlockSpec((r, d), lambda l, j: (0, 0)),
            pl.BlockSpec((1, d, bn), lambda l, j: (l, 0, j)),
            pl.BlockSpec((1, 1, bn), lambda l, j: (l, 0, j)),
        ],
        out_specs=pl.BlockSpec((1, r, bn), lambda l, j: (l, 0, j)),
        out_shape=jax.ShapeDtypeStruct((depth, r, n), F32),
        compiler_params=_cparams(("arbitrary", "arbitrary")),
        name="mod",
    )(c_all, w_mod, b_mod.reshape(depth, 1, n))


def _inproj_kernel(*refs, n, emit_v):
    if emit_v:
        (x_ref, sh_ref, sc_ref, g_ref, wa_ref, wq_ref, wkv_ref, wz_ref, wg_ref, lg_ref, lb_ref, cw_ref, cb_ref,
         pa_ref, qb_ref, kv_ref, oc_ref, ga_ref, v_ref) = refs
    else:
        (x_ref, sh_ref, sc_ref, g_ref, wa_ref, wq_ref, wkv_ref, wz_ref, wg_ref, lg_ref, lb_ref, cw_ref, cb_ref,
         pa_ref, qb_ref, kv_ref, oc_ref, ga_ref) = refs
    tm = x_ref.shape[0]
    h = _modulate(x_ref[...], g_ref[...], sh_ref[...], sc_ref[...]).astype(BF16)
    col = 512

    def proj(w_ref, o_ref):
        for j in range(0, w_ref.shape[1], col):
            o_ref[:, j:j + col] = _dot(h, w_ref[:, j:j + col]).astype(o_ref.dtype)

    zs = []
    for j in range(0, 2 * CMLP_WIDTH, col):
        z = _dot(h, wz_ref[:, j:j + col])
        zs.append(0.5 * z * (1.0 + jnp.tanh(0.7978845608028654 * (z + 0.044715 * (z * z * z)))))
    u = jnp.concatenate(zs[:len(zs) // 2], axis=1)
    v = jnp.concatenate(zs[len(zs) // 2:], axis=1)
    proj(wa_ref, pa_ref)
    xc = v - jnp.mean(v, axis=-1, keepdims=True)
    var = jnp.mean(xc * xc, axis=-1, keepdims=True)
    vn = xc * lax.rsqrt(var + EPS) * lg_ref[...] + lb_ref[...]
    if emit_v:
        v_ref[...] = vn
    vb = vn.astype(BF16)
    for ch in range(tm // n):
        rs = slice(ch * n, (ch + 1) * n)
        for g in range(CMLP_GROUPS):
            cs = slice(g * CMLP_GW, (g + 1) * CMLP_GW)
            mix = _dot(cw_ref[g], vb[rs, cs]) + cb_ref[:, cs]
            oc_ref[rs, cs] = (u[rs, cs] * mix).astype(oc_ref.dtype)
    proj(wq_ref, qb_ref)
    proj(wkv_ref, kv_ref)
    ga_ref[...] = _dot(h, wg_ref[...])


def _row_spec(tm, width, per_seq_tiles):
    if per_seq_tiles is None:
        return pl.BlockSpec((tm, width), lambda i: (i, 0))
    return pl.BlockSpec((None, 1, width), lambda i: (i // per_seq_tiles, 0, 0))


def _const_spec(shape):
    nd = len(shape)
    return pl.BlockSpec(shape, lambda i: (0,) * nd, pipeline_mode=pl.Buffered(1))


def _inproj_call(x, sh, sc, g, wa, wq, wkv, wz, wg, lg, lb, cw, cb, tm, per_seq_tiles, n, emit_v):
    ntok, d = x.shape
    outs = [
        jax.ShapeDtypeStruct((ntok, wa.shape[1]), BF16),
        jax.ShapeDtypeStruct((ntok, wq.shape[1]), BF16),
        jax.ShapeDtypeStruct((ntok, wkv.shape[1]), F32),
        jax.ShapeDtypeStruct((ntok, CMLP_WIDTH), BF16),
        jax.ShapeDtypeStruct((ntok, LANES), F32),
    ]
    if emit_v:
        outs.append(jax.ShapeDtypeStruct((ntok, CMLP_WIDTH), F32))
    res = pl.pallas_call(
        functools.partial(_inproj_kernel, n=n, emit_v=emit_v),
        grid=(ntok // tm,),
        in_specs=[
            pl.BlockSpec((tm, d), lambda i: (i, 0)),
            _row_spec(tm, d, per_seq_tiles),
            _row_spec(tm, d, per_seq_tiles),
            _const_spec((1, d)),
            _const_spec(wa.shape), _const_spec(wq.shape), _const_spec(wkv.shape),
            _const_spec(wz.shape), _const_spec(wg.shape),
            _const_spec((1, CMLP_WIDTH)), _const_spec((1, CMLP_WIDTH)),
            _const_spec(cw.shape), _const_spec(cb.shape),
        ],
        out_specs=[pl.BlockSpec((tm, o.shape[1]), lambda i: (i, 0)) for o in outs],
        out_shape=outs,
        compiler_params=_cparams(("arbitrary",)),
        name="inproj",
    )(x, sh, sc, g, wa, wq, wkv, wz, wg, lg, lb, cw, cb)
    return res if emit_v else (*res, None)


def _gla_kernel(*refs, nblk, has_state):
    if has_state:
        (q_ref, k_ref, v_ref, r_ref, ga_ref, wg_ref, bg_ref, ng_ref, s0_ref,
         o_ref, sfin_ref, st_ref) = refs
    else:
        (q_ref, k_ref, v_ref, r_ref, ga_ref, wg_ref, bg_ref, ng_ref,
         o_ref, sfin_ref, st_ref) = refs
        s0_ref = None
    t = pl.program_id(2)

    @pl.when(t == 0)
    def _():
        if has_state:
            st_ref[...] = s0_ref[0, 0].T
        else:
            st_ref[...] = jnp.zeros_like(st_ref)

    row = lax.broadcasted_iota(jnp.int32, (CHUNK, CHUNK), 0)
    colm = lax.broadcasted_iota(jnp.int32, (CHUNK, CHUNK), 1)
    causal = row >= colm
    tril = causal.astype(F32)
    gl = _dot(ga_ref[...].astype(BF16), wg_ref[...].astype(BF16)) + bg_ref[...]
    la = (jnp.minimum(gl, 0.0) - jnp.log(1.0 + jnp.exp(-jnp.abs(gl)))) / GLA_TAU
    sls = [slice(c * CHUNK, (c + 1) * CHUNK) for c in range(nblk)]
    cum = [jnp.dot(tril, la[sl], precision=HIGHEST, preferred_element_type=F32) for sl in sls]
    qts, kts, kds, decs = [], [], [], []
    for c, sl in enumerate(sls):
        b = cum[c]
        bl = b[CHUNK - 1:CHUNK, :]
        q = q_ref[sl, :].astype(F32) * (GLA_DK ** -0.5)
        k = k_ref[sl, :].astype(F32)
        qts.append((q * jnp.exp(b)).astype(BF16))
        kts.append((k * jnp.exp(-b)).astype(BF16))
        kds.append((k * jnp.exp(bl - b)).astype(BF16))
        decs.append(jnp.exp(bl))
    amats = [jnp.where(causal, _dot_nt(qts[c], kts[c]), 0.0).astype(BF16) for c in range(nblk)]
    kvs = [_dot_tn(v_ref[sls[c], :], kds[c]) for c in range(nblk)]
    st = st_ref[...]
    states = []
    for c in range(nblk):
        states.append(st.astype(BF16))
        st = st * decs[c] + kvs[c]
    st_ref[...] = st
    for c, sl in enumerate(sls):
        o = _dot(amats[c], v_ref[sl, :]) + _dot_nt(qts[c], states[c])
        on = o * lax.rsqrt(jnp.mean(o * o, axis=-1, keepdims=True) + EPS) * ng_ref[...]
        r = r_ref[sl, :].astype(F32)
        o_ref[sl, :] = (on * (r * _sigmoid(r))).astype(o_ref.dtype)

    @pl.when(t == pl.num_programs(2) - 1)
    def _():
        sfin_ref[0, 0] = st_ref[...].T


def _gla_call(pa, ga, wg, bg, ng, s0, nseq, seqlen, tb):
    n = pa.shape[0]
    nt = seqlen // tb
    kq = GLA_HEADS
    vq = (2 * GLA_HEADS * GLA_DK) // GLA_DV
    rq = vq + GLA_HEADS
    in_specs = [
        pl.BlockSpec((tb, GLA_DK), lambda b, h, t: (b * nt + t, h)),
        pl.BlockSpec((tb, GLA_DK), lambda b, h, t: (b * nt + t, kq + h)),
        pl.BlockSpec((tb, GLA_DV), lambda b, h, t: (b * nt + t, vq + h)),
        pl.BlockSpec((tb, GLA_DV), lambda b, h, t: (b * nt + t, rq + h)),
        pl.BlockSpec((tb, LANES), lambda b, h, t: (b * nt + t, 0)),
        pl.BlockSpec((LANES, GLA_DK), lambda b, h, t: (0, h)),
        pl.BlockSpec((1, GLA_DK), lambda b, h, t: (0, h)),
        pl.BlockSpec((1, GLA_DV), lambda b, h, t: (0, 0)),
    ]
    args = [pa, pa, pa, pa, ga, wg, bg, ng]
    if s0 is not None:
        in_specs.append(pl.BlockSpec((1, 1, GLA_DK, GLA_DV), lambda b, h, t: (b, h, 0, 0)))
        args.append(s0)
    return pl.pallas_call(
        functools.partial(_gla_kernel, nblk=tb // CHUNK, has_state=s0 is not None),
        grid=(nseq, GLA_HEADS, nt),
        in_specs=in_specs,
        out_specs=[
            pl.BlockSpec((tb, GLA_DV), lambda b, h, t: (b * nt + t, h)),
            pl.BlockSpec((1, 1, GLA_DK, GLA_DV), lambda b, h, t: (b, h, 0, 0)),
        ],
        out_shape=(
            jax.ShapeDtypeStruct((n, GLA_HEADS * GLA_DV), BF16),
            jax.ShapeDtypeStruct((nseq, GLA_HEADS, GLA_DK, GLA_DV), F32),
        ),
        scratch_shapes=[pltpu.VMEM((GLA_DV, GLA_DK), F32)],
        compiler_params=_cparams(("arbitrary", "arbitrary", "arbitrary")),
        name="gla",
    )(*args)


def _swa_kernel(q_ref, kc_ref, vc_ref, kp_ref, vp_ref, bias_ref, sink_ref, o_ref, *, tq, mask_start):
    blk = pl.program_id(1)
    kall = jnp.concatenate([kp_ref[...], kc_ref[...]], axis=0).astype(BF16)
    vall = jnp.concatenate([vp_ref[...], vc_ref[...]], axis=0).astype(BF16)
    lwin = WINDOW + CHUNK
    cols = SWA_GROUP * CHUNK
    for c in range(tq // CHUNK):
        need_mask = mask_start and c * CHUNK < WINDOW
        if need_mask:
            first_pos = blk * tq + c * CHUNK - WINDOW
            valid = lax.broadcasted_iota(jnp.int32, (lwin, cols), 0) + first_pos >= 0
        scores = []
        for x in range(SWA_KV_HEADS):
            kw = kall[c * CHUNK:c * CHUNK + lwin, x * SWA_HD:(x + 1) * SWA_HD]
            base = x * SWA_GROUP * SWA_HD
            qs = jnp.concatenate(
                [q_ref[c * CHUNK:(c + 1) * CHUNK, base + g * SWA_HD:base + (g + 1) * SWA_HD]
                 for g in range(SWA_GROUP)], axis=0)
            scores.append(_dot_nt(kw, qs))
        probs = []
        for x in range(SWA_KV_HEADS):
            s = scores[x] + bias_ref[x]
            if need_mask:
                s = jnp.where(valid, s, -1e30)
            sink = sink_ref[x]
            m = jnp.maximum(jnp.max(s, axis=0, keepdims=True), sink)
            p = jnp.exp(s - m)
            den = jnp.sum(p, axis=0, keepdims=True) + jnp.exp(sink - m)
            probs.append((p.astype(BF16), 1.0 / den))
        for x in range(SWA_KV_HEADS):
            vw = vall[c * CHUNK:c * CHUNK + lwin, x * SWA_HD:(x + 1) * SWA_HD]
            base = x * SWA_GROUP * SWA_HD
            vpad = jnp.concatenate([vw, jnp.zeros_like(vw)], axis=1)
            p, rden = probs[x]
            o = (_dot_tn(vpad, p) * rden).T
            o_ref[c * CHUNK:(c + 1) * CHUNK, base:base + SWA_GROUP * SWA_HD] = jnp.concatenate(
                [o[g * CHUNK:(g + 1) * CHUNK, :SWA_HD] for g in range(SWA_GROUP)], axis=1).astype(o_ref.dtype)


def _swa_call(qb, k_cur, v_cur, k_prev, v_prev, bias, sink, nseq, seqlen, tq, prompt):
    n = qb.shape[0]
    nt = seqlen // tq
    kvw = SWA_KV_HEADS * SWA_HD
    per_prev = tq // WINDOW if prompt else None

    def cur_spec(cb):
        return pl.BlockSpec((tq, kvw), lambda b, t: (b * nt + t, cb))

    def prev_spec(cb):
        if prompt:
            nprev = seqlen // WINDOW
            return pl.BlockSpec((WINDOW, kvw),
                                lambda b, t: (b * nprev + jnp.maximum(t * per_prev - 1, 0), cb))
        return pl.BlockSpec((None, WINDOW, kvw), lambda b, t: (b, 0, cb))

    return pl.pallas_call(
        functools.partial(_swa_kernel, tq=tq, mask_start=prompt),
        grid=(nseq, nt),
        in_specs=[
            pl.BlockSpec((tq, SWA_HEADS * SWA_HD), lambda b, t: (b * nt + t, 0)),
            cur_spec(k_cur[1]), cur_spec(v_cur[1]), prev_spec(k_prev[1]), prev_spec(v_prev[1]),
            pl.BlockSpec(bias.shape, lambda b, t: (0, 0, 0)),
            pl.BlockSpec(sink.shape, lambda b, t: (0, 0, 0)),
        ],
        out_specs=pl.BlockSpec((tq, SWA_HEADS * SWA_HD), lambda b, t: (b * nt + t, 0)),
        out_shape=jax.ShapeDtypeStruct((n, SWA_HEADS * SWA_HD), BF16),
        compiler_params=_cparams(("arbitrary", "arbitrary")),
        name="swa",
    )(qb, k_cur[0], v_cur[0], k_prev[0], v_prev[0], bias, sink)


def _merge_kernel(x_ref, sh1_ref, sc1_ref, gt1_ref, sh2_ref, sc2_ref, g1_ref, g2_ref,
                  oa_ref, ob_ref, oc_ref, wmg_ref, bmg_ref, wa_ref, wb_ref, wc_ref, wo_ref,
                  wrh_ref, wrw_ref, br_ref, xo_ref, h2_ref, lg_ref):
    x = x_ref[...]
    h = _modulate(x, g1_ref[...], sh1_ref[...], sc1_ref[...]).astype(BF16)
    d = x.shape[1]
    m = None
    for i, (o_ref, w_ref) in enumerate(((oa_ref, wa_ref), (ob_ref, wb_ref), (oc_ref, wc_ref))):
        gate = _sigmoid(_dot(h, wmg_ref[:, i * d:(i + 1) * d]) + bmg_ref[:, i * d:(i + 1) * d])
        term = gate * _dot(o_ref[...], w_ref[...])
        m = term if m is None else m + term
    xn = x + gt1_ref[...] * _dot(m.astype(BF16), wo_ref[...])
    xo_ref[...] = xn
    h2 = _modulate(xn, g2_ref[...], sh2_ref[...], sc2_ref[...])
    hi = h2.astype(BF16)
    hf = hi.astype(F32)
    lo = (h2 - hf).astype(BF16)
    h2_ref[...] = ((lax.bitcast_convert_type(hf[:, :d // 2], U32) >> 16)
                   | (lax.bitcast_convert_type(hf[:, d // 2:], U32) & jnp.uint32(0xFFFF0000)))
    both = _dot(hi, wrw_ref[...])
    lg_ref[...] = (both[:, :ROUTE_W] + both[:, ROUTE_W:] + _dot(lo, wrh_ref[...])) + br_ref[...]


def _merge_call(x, mods, g1, g2, oa, ob, oc, wmg, bmg, wa, wb, wc, wo, wrh, wrw, br, tm, per_seq_tiles):
    n, d = x.shape
    tok = lambda w: pl.BlockSpec((tm, w), lambda i: (i, 0))
    in_specs = [tok(d)] + [_row_spec(tm, d, per_seq_tiles) for _ in mods]
    in_specs += [_const_spec((1, d)), _const_spec((1, d)), tok(d), tok(d), tok(d)]
    in_specs += [_const_spec(a.shape) for a in (wmg, bmg, wa, wb, wc, wo, wrh, wrw, br)]
    return pl.pallas_call(
        _merge_kernel,
        grid=(n // tm,),
        in_specs=in_specs,
        out_specs=[tok(d), tok(d // 2), tok(ROUTE_W)],
        out_shape=(
            jax.ShapeDtypeStruct((n, d), F32),
            jax.ShapeDtypeStruct((n, d // 2), U32),
            jax.ShapeDtypeStruct((n, ROUTE_W), F32),
        ),
        compiler_params=_cparams(("arbitrary",)),
        name="merge",
    )(x, *mods, g1, g2, oa, ob, oc, wmg, bmg, wa, wb, wc, wo, wrh, wrw, br)


def _route_kernel(lg_ref, pos_ref, w_ref, meta_ref):
    tm = lg_ref.shape[0]
    ru = -(-(N_EXPERTS + MOE_GROUPS) // SUBLANES) * SUBLANES
    lgt = lg_ref[...].T[:ru, :]
    row = lax.broadcasted_iota(jnp.int32, lgt.shape, 0)
    big = jnp.int32(1 << 20)
    ninf = -jnp.inf
    is_g = (row >= N_EXPERTS) & (row < N_EXPERTS + MOE_GROUPS)
    gl = jnp.where(is_g, lgt, ninf)
    gmax = jnp.max(gl, axis=0, keepdims=True)
    g_idx = jnp.min(jnp.where(gl == gmax, row - N_EXPERTS, big), axis=0, keepdims=True)
    g_val = 1.0 / jnp.sum(jnp.exp(gl - gmax), axis=0, keepdims=True)
    in_grp = (row < N_EXPERTS) & ((row // MOE_PER_GROUP) == g_idx)
    el = jnp.where(in_grp, lgt, ninf)
    v0 = jnp.max(el, axis=0, keepdims=True)
    i0 = jnp.min(jnp.where(el == v0, row, big), axis=0, keepdims=True)
    el1 = jnp.where(row == i0, ninf, el)
    v1 = jnp.max(el1, axis=0, keepdims=True)
    i1 = jnp.min(jnp.where(el1 == v1, row, big), axis=0, keepdims=True)
    e10 = jnp.exp(v1 - v0)
    w0 = g_val / (1.0 + e10)
    w1 = g_val * e10 / (1.0 + e10)

    hit0 = row == i0
    hit1 = row == i1
    ab = (hit0 | hit1).astype(BF16)
    sb = min(tm, 256)
    r_i = lax.broadcasted_iota(jnp.int32, (sb, sb), 0)
    c_i = lax.broadcasted_iota(jnp.int32, (sb, sb), 1)
    before = (r_i < c_i).astype(BF16)
    ranks = []
    cnt = jnp.zeros((ru, 1), F32)
    for q in range(tm // sb):
        a_q = ab[:, q * sb:(q + 1) * sb]
        ranks.append(_dot(a_q, before) + cnt)
        cnt = cnt + jnp.sum(a_q.astype(F32), axis=1, keepdims=True)
    rank = jnp.concatenate(ranks, axis=1)
    cnt_b = jnp.broadcast_to(cnt, (ru, ROUTE_W))
    t_r = lax.broadcasted_iota(jnp.int32, (ru, ru), 0)
    t_c = lax.broadcasted_iota(jnp.int32, (ru, ru), 1)
    off_b = jnp.dot((t_c < t_r).astype(F32), cnt_b, precision=HIGHEST, preferred_element_type=F32)
    slot = rank + off_b[:, 0:1]
    p0 = jnp.sum(jnp.where(hit0, slot, 0.0), axis=0, keepdims=True)
    p1 = jnp.sum(jnp.where(hit1, slot, 0.0), axis=0, keepdims=True)
    row8 = lax.broadcasted_iota(jnp.int32, (SUBLANES, tm), 0)
    pos_ref[...] = jnp.where(row8 == 0, p0, jnp.where(row8 == 1, p1, 0.0)).astype(jnp.int32)
    roww = lax.broadcasted_iota(jnp.int32, (ROUTE_W, tm), 0)
    w_ref[...] = jnp.where(roww == 0, w0, jnp.where(roww == 1, w1, 0.0)).T
    pad = jnp.zeros((ROUTE_W - ru, ROUTE_W), F32)
    off_l = jnp.concatenate([off_b, pad], axis=0).T[:SUBLANES, :]
    cnt_l = jnp.concatenate([cnt_b, pad], axis=0).T[:SUBLANES, :]
    r8 = lax.broadcasted_iota(jnp.int32, (SUBLANES, ROUTE_W), 0)
    meta_ref[0] = jnp.where(r8 == 0, off_l, cnt_l).astype(jnp.int32)


def _route_call(lg, tm):
    n = lg.shape[0]
    nt = n // tm
    return pl.pallas_call(
        _route_kernel,
        grid=(nt,),
        in_specs=[pl.BlockSpec((tm, ROUTE_W), lambda i: (i, 0))],
        out_specs=[
            pl.BlockSpec((SUBLANES, tm), lambda i: (i, 0)),
            pl.BlockSpec((tm, ROUTE_W), lambda i: (i, 0)),
            pl.BlockSpec((1, SUBLANES, ROUTE_W), lambda i: (i, 0, 0)),
        ],
        out_shape=(
            jax.ShapeDtypeStruct((nt * SUBLANES, tm), jnp.int32),
            jax.ShapeDtypeStruct((n, ROUTE_W), F32),
            jax.ShapeDtypeStruct((nt, SUBLANES, ROUTE_W), jnp.int32),
        ),
        compiler_params=_cparams(("arbitrary",)),
        name="route",
    )(lg)


def _moe_kernel(pos_ref, meta_ref, h_ref, x_ref, w2_ref, gt_ref, fg_ref, wg_ref, wu_ref, wd_ref,
                o_ref, xt_ref, ct0_ref, ct1_ref, slot0_ref, slot1_ref, *, tm, c, eps, nt, final):
    g = pl.program_id(0)
    e = pl.program_id(1)
    nq = D_MODEL // 2 // LANES
    half = D_MODEL // 2
    nsteps = N_EXPERTS // eps
    half_steps = nsteps // 2
    rows_step = tm // half_steps
    per_exp = rows_step // eps
    rc = min(per_exp, 256)
    nslot = 2 * tm
    blocks_on = jnp.logical_and(g >= 1, g <= nt).astype(jnp.int32)
    mbase = jnp.clip(g - 1, 0, nt - 1) * (2 * N_EXPERTS)
    pcomb = jnp.clip(g - 2, 0, nt - 1) * nslot
    pdisp = jnp.minimum(g, nt - 1) * nslot

    def read_rows(ref, r0, nrows):
        return [ref[pl.ds(r0 + s, nrows, stride=nq), :] for s in range(nq)]

    def words_to_f32(ws):
        los = [lax.bitcast_convert_type(w << 16, F32) for w in ws]
        his = [lax.bitcast_convert_type(w & jnp.uint32(0xFFFF0000), F32) for w in ws]
        return jnp.concatenate(los + his, axis=1)

    @pl.when(jnp.logical_and(g == 0, e == 0))
    def _():
        slot0_ref[...] = jnp.zeros(slot0_ref.shape, U32)
        slot1_ref[...] = jnp.zeros(slot1_ref.shape, U32)

    def stage(cur_ref, oth_ref):
        def block(b, j, start, cnt):
            r0 = pl.multiple_of((start + b * c) * nq, nq)
            ws = read_rows(cur_ref, r0, c)
            xb = words_to_f32(ws).astype(BF16)
            gate = _dot(xb, wg_ref[j])
            hid = (gate * _sigmoid(gate)) * _dot(xb, wu_ref[j])
            yb = _dot(hid.astype(BF16), wd_ref[j]).astype(BF16).astype(F32)
            yw = ((lax.bitcast_convert_type(yb[:, :half], U32) >> 16)
                  | (lax.bitcast_convert_type(yb[:, half:], U32) & jnp.uint32(0xFFFF0000)))
            keep = lax.broadcasted_iota(jnp.int32, (c, 1), 0) < (cnt - b * c)
            for s in range(nq):
                cur_ref[pl.ds(r0 + s, c, stride=nq), :] = jnp.where(
                    keep, yw[:, s * LANES:(s + 1) * LANES], ws[s])

        def expert(j, copies):
            ex = e * eps + j
            start = meta_ref[mbase + ex]
            cnt = meta_ref[mbase + N_EXPERTS + ex] * blocks_on
            block(0, j, start, cnt)
            copies(j)

            def more(b, carry):
                block(b, j, start, cnt)
                return carry
            lax.fori_loop(1, (cnt + c - 1) // c, more, 0)

        @pl.when(e < half_steps)
        def _():
            tok0 = e * rows_step

            def gather(j):
                lo = j * per_exp
                for tl in range(lo, lo + per_exp):
                    for k, ct_ref in ((0, ct0_ref), (1, ct1_ref)):
                        p = pos_ref[pcomb + k * tm + tok0 + tl]
                        ct_ref[pl.ds(tl * nq, nq), :] = oth_ref[pl.ds(pl.multiple_of(p * nq, nq), nq), :]
                for r in range(lo, lo + per_exp, rc):
                    rows = slice(r, r + rc)
                    wts = w2_ref[rows, :]
                    acc = (words_to_f32(read_rows(ct0_ref, r * nq, rc)) * wts[:, 0:1]
                           + words_to_f32(read_rows(ct1_ref, r * nq, rc)) * wts[:, 1:2])
                    gate = gt_ref[...] if gt_ref.shape[0] == 1 else gt_ref[rows, :]
                    xn = x_ref[rows, :] + gate * acc
                    if final:
                        xn = xn * lax.rsqrt(jnp.mean(xn * xn, axis=-1, keepdims=True) + EPS) * fg_ref[...]
                    o_ref[rows, :] = xn
            for j in range(eps):
                expert(j, gather)

        @pl.when(e >= half_steps)
        def _():
            @pl.when(e == half_steps)
            def _():
                for s in range(nq):
                    xt_ref[pl.ds(s, tm, stride=nq), :] = h_ref[:, s * LANES:(s + 1) * LANES]
                oth_ref[pl.ds(nslot * nq, c * nq), :] = jnp.zeros((c * nq, LANES), U32)
            tok0 = (e - half_steps) * rows_step

            def scatter(j):
                for tl in range(j * per_exp, (j + 1) * per_exp):
                    t = tok0 + tl
                    row = xt_ref[pl.ds(pl.multiple_of(t * nq, nq), nq), :]
                    for k in range(2):
                        p = pos_ref[pdisp + k * tm + t]
                        oth_ref[pl.ds(pl.multiple_of(p * nq, nq), nq), :] = row
            for j in range(eps):
                expert(j, scatter)

    @pl.when(g % 2 == 1)
    def _():
        stage(slot0_ref, slot1_ref)

    @pl.when(g % 2 == 0)
    def _():
        stage(slot1_ref, slot0_ref)


def _moe_call(pos, meta, h2, x, w2, gt, fg, wg, wu, wd, tm, c, eps, per_seq_tiles, final):
    n, d = x.shape
    nt = n // tm
    nq = d // 2 // LANES
    nsteps = N_EXPERTS // eps
    rows_step = tm // (nsteps // 2)

    half_steps = nsteps // 2

    def done(g, e):
        return jnp.where(g >= 2, jnp.minimum(g - 2, nt - 1) * half_steps + jnp.minimum(e, half_steps - 1), 0)

    def incoming(g, e):
        return jnp.clip(jnp.where(e >= half_steps - 1, g, g - 1), 0, nt - 1)

    if per_seq_tiles is None:
        gt_spec = pl.BlockSpec((rows_step, d), lambda g, e, *_: (done(g, e), 0))
    else:
        gt_spec = pl.BlockSpec((None, 1, d),
                               lambda g, e, *_: (jnp.clip(g - 2, 0, nt - 1) // per_seq_tiles, 0, 0))
    grid_spec = pltpu.PrefetchScalarGridSpec(
        num_scalar_prefetch=2,
        grid=(nt + 2, nsteps),
        in_specs=[
            pl.BlockSpec((tm, d // 2), lambda g, e, *_: (incoming(g, e), 0)),
            pl.BlockSpec((rows_step, d), lambda g, e, *_: (done(g, e), 0)),
            pl.BlockSpec((rows_step, ROUTE_W), lambda g, e, *_: (done(g, e), 0)),
            gt_spec,
            pl.BlockSpec((1, d), lambda g, e, *_: (0, 0)),
            pl.BlockSpec((eps, d, MOE_HIDDEN), lambda g, e, *_: (e, 0, 0)),
            pl.BlockSpec((eps, d, MOE_HIDDEN), lambda g, e, *_: (e, 0, 0)),
            pl.BlockSpec((eps, MOE_HIDDEN, d), lambda g, e, *_: (e, 0, 0)),
        ],
        out_specs=pl.BlockSpec((rows_step, d), lambda g, e, *_: (done(g, e), 0)),
        scratch_shapes=[
            pltpu.VMEM((tm * nq, LANES), U32),
            pltpu.VMEM((rows_step * nq, LANES), U32),
            pltpu.VMEM((rows_step * nq, LANES), U32),
            pltpu.VMEM(((2 * tm + c) * nq, LANES), U32),
            pltpu.VMEM(((2 * tm + c) * nq, LANES), U32),
        ],
    )
    return pl.pallas_call(
        functools.partial(_moe_kernel, tm=tm, c=c, eps=eps, nt=nt, final=final),
        grid_spec=grid_spec,
        out_shape=jax.ShapeDtypeStruct((n, d), F32),
        compiler_params=_cparams(("arbitrary", "arbitrary")),
        name="moe",
    )(pos, meta, h2, x, w2, gt, fg, wg, wu, wd)


def _rel_bucket(rel):
    nb = REL_BUCKETS // 2
    max_exact = nb // 2
    ret = jnp.where(rel > 0, nb, 0)
    nabs = jnp.abs(rel)
    nf = jnp.maximum(nabs, 1).astype(F32)
    large = max_exact + (jnp.log(nf / max_exact) / jnp.log(REL_MAX_DIST / max_exact)
                         * (nb - max_exact)).astype(jnp.int32)
    large = jnp.minimum(large, nb - 1)
    return ret + jnp.where(nabs < max_exact, nabs, large)


def _swa_bias(table):
    lwin = WINDOW + CHUNK
    rel = jnp.arange(lwin)[None, :] - WINDOW - jnp.arange(CHUNK)[:, None]
    onehot = (_rel_bucket(rel)[..., None] == jnp.arange(REL_BUCKETS)).astype(F32)
    b = jnp.einsum('qlb,bh->hql', onehot, table.astype(F32), precision=HIGHEST)
    b = b.reshape(SWA_KV_HEADS, SWA_GROUP * CHUNK, lwin)
    return jnp.transpose(b, (0, 2, 1))


def _layer_params(l, p):
    d = D_MODEL
    w_in = p['w_in'][l]
    o = 0
    cols = {}
    for name, width in (('qa', 512), ('ka', 512), ('va', 1024), ('ga', GLA_RANK), ('ra', 1024),
                        ('qb', 1024), ('kb', 256), ('vb', 256), ('zc', 2048)):
        cols[name] = w_in[:, o:o + width]
        o += width
    lp = {}
    lp['wa'] = jnp.concatenate([cols['qa'], cols['ka'], cols['va'], cols['ra']], axis=1).astype(BF16)
    lp['wq'] = (cols['qb'] * (SWA_HD ** -0.5)).astype(BF16)
    lp['wkv'] = jnp.concatenate([cols['kb'], cols['vb']], axis=1).astype(BF16)
    lp['wz'] = cols['zc'].astype(BF16)
    lp['wg'] = jnp.pad(cols['ga'], ((0, 0), (0, LANES - GLA_RANK))).astype(BF16)
    lp['gla_wg'] = jnp.pad(p['gla_w_gate'][l], ((0, LANES - GLA_RANK), (0, 0)))
    lp['gla_bg'] = p['gla_b_gate'][l][None, :]
    lp['gla_ng'] = p['gla_norm_g'][l][None, :]
    sink = p['swa_sink'][l].astype(F32).reshape(SWA_KV_HEADS, SWA_GROUP, 1)
    lp['sink'] = jnp.broadcast_to(sink[:, :, None, :], (SWA_KV_HEADS, SWA_GROUP, CHUNK, 1)).reshape(
        SWA_KV_HEADS, 1, SWA_GROUP * CHUNK)
    lp['ln_g'] = p['cmlp_ln_g'][l][None, :]
    lp['ln_b'] = p['cmlp_ln_b'][l][None, :]
    ws = p['cmlp_ws'][l]
    pos = jnp.arange(CMLP_CHUNK)
    mask = (pos[None, :] // CHUNK) <= (pos[:, None] // CHUNK)
    lp['cw_full'] = jnp.where(mask[None], ws, 0.0).astype(BF16)
    lp['cw_half'] = ws[:, :CHUNK, :CHUNK].astype(BF16)
    bs = p['cmlp_bs'][l]
    bs_full = jnp.repeat(bs.T, CMLP_GW, axis=1).astype(F32)
    lp['cb_full'] = bs_full
    lp['cb_half'] = bs_full[:CHUNK]
    lp['wmg'] = p['w_merge_gate'][l].astype(BF16)
    lp['bmg'] = p['b_merge_gate'][l][None, :]
    lp['wba'] = p['w_branch_a'][l].astype(BF16)
    lp['wbb'] = p['w_branch_b'][l].astype(BF16)
    lp['wbc'] = p['w_branch_c'][l].astype(BF16)
    lp['wo'] = p['w_out'][l].astype(BF16)
    wr = jnp.concatenate([p['moe_w_re'][l], p['moe_w_rg'][l]], axis=1)
    wr = jnp.pad(wr, ((0, 0), (0, ROUTE_W - wr.shape[1])))
    wrh = wr.astype(BF16)
    lp['wrh'] = wrh
    lp['wrw'] = jnp.concatenate([wrh, (wr - wrh.astype(F32)).astype(BF16)], axis=1)
    br = jnp.concatenate([p['moe_b_re'][l], p['moe_b_rg'][l]])
    lp['br'] = jnp.pad(br, (0, ROUTE_W - br.shape[0]))[None, :].astype(F32)
    lp['ewg'] = p['moe_w_gate'][l].astype(BF16)
    lp['ewu'] = p['moe_w_up'][l].astype(BF16)
    lp['ewd'] = p['moe_w_down'][l].astype(BF16)
    lp['n1'] = p['norm1_g'][l][None, :]
    lp['n2'] = p['norm2_g'][l][None, :]
    return lp


def _pick_tile(n, pref):
    t = min(n, pref)
    while n % t:
        t //= 2
    return t


def _run_group(x3, mod, lps, bias, final_g, prompt, state_gla, cache_k, cache_v):
    nseq, seqlen, d = x3.shape
    n = nseq * seqlen
    x = x3.reshape(n, d)
    depth = len(lps)
    tm = _pick_tile(seqlen, TM_PROJ) if prompt else _pick_tile(n, TM_PROJ // 2)
    tm_moe = _pick_tile(seqlen if prompt else n, TM_MOE)
    kvw = SWA_KV_HEADS * SWA_HD
    gla_out, k_out, v_out, vc_out = [], [], [], []
    for l, lp in enumerate(lps):
        m6 = mod[l].reshape(nseq, 6, d)
        if prompt:
            mods = [m6[:, j][:, None, :] for j in range(6)]
            pst = seqlen // tm
            pst_moe = seqlen // tm_moe
        else:
            mods = [jnp.broadcast_to(m6[:, j][:, None, :], (nseq, seqlen, d)).reshape(n, d) for j in range(6)]
            pst = None
            pst_moe = None
        sh1, sc1, gt1, sh2, sc2, gt2 = mods
        if prompt:
            cw, cb, nch = lp['cw_full'], lp['cb_full'], CMLP_CHUNK
        else:
            cw, cb, nch = lp['cw_half'], lp['cb_half'], seqlen
        pa, qb, kv, oc, ga, vc = _inproj_call(x, sh1, sc1, lp['n1'], lp['wa'], lp['wq'], lp['wkv'], lp['wz'],
                                              lp['wg'], lp['ln_g'], lp['ln_b'], cw, cb, tm, pst, nch, not prompt)
        if not prompt:
            vc_out.append(vc.reshape(nseq, seqlen, CMLP_WIDTH))
        tb = _pick_tile(seqlen, TB_GLA)
        s0 = None if prompt else state_gla[l]
        oa, s_new = _gla_call(pa, ga, lp['gla_wg'], lp['gla_bg'], lp['gla_ng'], s0, nseq, seqlen, tb)
        if prompt:
            tq = _pick_tile(seqlen, TQ_SWA)
            ob = _swa_call(qb, (kv, 0), (kv, 1), (kv, 0), (kv, 1), bias, lp['sink'], nseq, seqlen, tq, True)
            kv3 = kv.reshape(nseq, seqlen, 2, SWA_KV_HEADS, SWA_HD)
            k_state = kv3[:, -WINDOW:, 0]
            v_state = kv3[:, -WINDOW:, 1]
        else:
            lc = cache_k.shape[2]
            ck = cache_k[l].reshape(nseq, lc, kvw)
            cv = cache_v[l].reshape(nseq, lc, kvw)
            ob = _swa_call(qb, (kv, 0), (kv, 1), (ck, 0), (cv, 0), bias, lp['sink'], nseq, seqlen, seqlen, False)
            kv3 = kv.reshape(nseq, seqlen, 2, SWA_KV_HEADS, SWA_HD)
            k_state = jnp.concatenate([cache_k[l], kv3[:, :, 0]], axis=1)[:, -lc:]
            v_state = jnp.concatenate([cache_v[l], kv3[:, :, 1]], axis=1)[:, -lc:]
        x1, h2, lg = _merge_call(x, [sh1, sc1, gt1, sh2, sc2], lp['n1'], lp['n2'], oa, ob, oc,
                                 lp['wmg'], lp['bmg'], lp['wba'], lp['wbb'], lp['wbc'], lp['wo'],
                                 lp['wrh'], lp['wrw'], lp['br'], tm, pst)
        pos, w2, meta = _route_call(lg, tm_moe)
        meta_flat = jnp.concatenate([meta[:, 0, :N_EXPERTS], meta[:, 1, :N_EXPERTS]], axis=1).reshape(-1)
        pos_flat = pos.reshape(-1, SUBLANES, tm_moe)[:, :2, :].reshape(-1)
        x = _moe_call(pos_flat, meta_flat, h2, x1, w2, gt2, final_g, lp['ewg'], lp['ewu'], lp['ewd'],
                      tm_moe, _moe_block_rows(tm_moe), EPS_MOE, pst_moe, l == depth - 1)
        gla_out.append(s_new)
        k_out.append(k_state)
        v_out.append(v_state)
    y = x.reshape(nseq, seqlen, d)
    vc_stack = jnp.stack(vc_out) if vc_out else None
    return y, jnp.stack(gla_out), jnp.stack(k_out), jnp.stack(v_out), vc_stack


def kernel(x_prompt, x_sample, state_gla, cache_swa_k, cache_swa_v, c_prompt, c_sample, rel_table, final_norm_g, norm1_g, norm2_g, w_mod, b_mod, w_in, gla_w_gate, gla_b_gate, gla_norm_g, swa_sink, cmlp_ln_g, cmlp_ln_b, cmlp_ws, cmlp_bs, w_branch_a, w_branch_b, w_branch_c, w_merge_gate, b_merge_gate, w_out, moe_w_rg, moe_b_rg, moe_w_re, moe_b_re, moe_w_gate, moe_w_up, moe_w_down):
    p = dict(norm1_g=norm1_g, norm2_g=norm2_g, w_in=w_in, gla_w_gate=gla_w_gate, gla_b_gate=gla_b_gate,
             gla_norm_g=gla_norm_g, swa_sink=swa_sink, cmlp_ln_g=cmlp_ln_g, cmlp_ln_b=cmlp_ln_b,
             cmlp_ws=cmlp_ws, cmlp_bs=cmlp_bs, w_branch_a=w_branch_a, w_branch_b=w_branch_b,
             w_branch_c=w_branch_c, w_merge_gate=w_merge_gate, b_merge_gate=b_merge_gate, w_out=w_out,
             moe_w_rg=moe_w_rg, moe_b_rg=moe_b_rg, moe_w_re=moe_w_re, moe_b_re=moe_b_re,
             moe_w_gate=moe_w_gate, moe_w_up=moe_w_up, moe_w_down=moe_w_down)
    depth = w_in.shape[0]
    nb = c_prompt.shape[0]
    lps = [_layer_params(l, p) for l in range(depth)]
    bias = _swa_bias(rel_table)
    mod = _mod_call(jnp.concatenate([c_prompt, c_sample], axis=0), w_mod, b_mod)
    fg = final_norm_g[None, :]
    y_p, gla_p, k_p, v_p, _ = _run_group(x_prompt, mod[:, :nb], lps, bias, fg, True, None, None, None)
    y_s, gla_s, k_s, v_s, vc_s = _run_group(x_sample, mod[:, nb:], lps, bias, fg, False,
                                            state_gla, cache_swa_k, cache_swa_v)
    return (y_p, y_s, gla_p, k_p, v_p, gla_s, k_s, v_s, vc_s)
```
